```python
import math
import numpy as np
import jax
import jax.numpy as jnp
from jax import lax

D_MODEL = 1024
BATCH = 8
SEQ = 8192
DEPTH = 4

CTX_LEN = 256
GRID_W = 64
HEAD_DIM = 64
N_BRANCH = 4
BRANCH_W = D_MODEL // 4
RET_HEADS = BRANCH_W // HEAD_DIM
RET_CHUNK = 128
NA_HEADS = BRANCH_W // HEAD_DIM
NA_KH = 8
NA_KW = 16
NA_QCB = 16
NA_KRW = NA_QCB + NA_KW
S5_GROUP_CH = 16
S5_GROUPS = BRANCH_W // S5_GROUP_CH
S5_STATE = 64
GQA_Q_HEADS = BRANCH_W // HEAD_DIM
GQA_KV_HEADS = GQA_Q_HEADS // 2
GQA_KV_W = GQA_KV_HEADS * HEAD_DIM
WIN = 128
ATT_BLOCK = 128
MOE_GROUPS = 4
EXPERTS_PER_GROUP = 8
N_EXPERTS = MOE_GROUPS * EXPERTS_PER_GROUP
TOP_K = 2
D_FF_EXPERT = D_MODEL // 2
MOE_BLOCK = 128
ROPE_BASE = 10000.0
EPS = 1e-6
NEG = -1e30
IN_SIZES = (BRANCH_W, BRANCH_W, BRANCH_W, BRANCH_W, BRANCH_W, BRANCH_W, BRANCH_W, BRANCH_W, BRANCH_W, GQA_KV_W, GQA_KV_W, N_BRANCH * D_MODEL)
MIX_COLS = 9 * BRANCH_W + 2 * GQA_KV_W
IN_COLS = MIX_COLS + N_BRANCH * D_MODEL

kernel_name = 'hybrid_flow_backbone'


def _rms(t, gain=None):
    tf = t.astype(jnp.float32)
    y = tf * lax.rsqrt(jnp.mean(tf * tf, axis=-1, keepdims=True) + EPS)
    if gain is not None:
        y = y * gain.astype(jnp.float32)
    return y.astype(t.dtype)


def _modulate(t, shift, scale):
    return t * (1.0 + scale) + shift


def _heads(t, n):
    return t.reshape(t.shape[0], t.shape[1], n, -1)


def _flip(t):
    return t[:, ::-1]


def _split(t, n_pieces):
    pts, acc = [], 0
    for s in IN_SIZES[:n_pieces - 1]:
        acc += s
        pts.append(acc)
    return jnp.split(t, pts, axis=-1)


def rope_tables(length):
    pos = jnp.arange(length, dtype=jnp.int32)
    n_freq = HEAD_DIM // 4
    inv = ROPE_BASE ** (-jnp.arange(n_freq, dtype=jnp.float32) / n_freq)
    ang_r = (pos // GRID_W).astype(jnp.float32)[:, None] * inv
    ang_c = (pos % GRID_W).astype(jnp.float32)[:, None] * inv
    return (jnp.cos(ang_r), jnp.sin(ang_r), jnp.cos(ang_c), jnp.sin(ang_c))


def _rope_1d(t, cos, sin):
    t1, t2 = jnp.split(t, 2, axis=-1)
    cos = cos[None, :, None, :].astype(t.dtype)
    sin = sin[None, :, None, :].astype(t.dtype)
    return jnp.concatenate([t1 * cos - t2 * sin, t1 * sin + t2 * cos], axis=-1)


def axial_rope(t, tabs):
    cr, sr, cc, sc = tabs
    half = t.shape[-1] // 2
    return jnp.concatenate([_rope_1d(t[..., :half], cr, sr), _rope_1d(t[..., half:], cc, sc)], axis=-1)


def retention_chunkwise(q, k, v, log_g, s0):
    b, length, h, dk = q.shape
    dv = v.shape[-1]
    nc = length // RET_CHUNK
    qc = q.astype(jnp.float32).reshape(b, nc, RET_CHUNK, h, dk)
    kc = k.astype(jnp.float32).reshape(b, nc, RET_CHUNK, h, dk)
    vc = v.astype(jnp.float32).reshape(b, nc, RET_CHUNK, h, dv)
    idx = jnp.arange(RET_CHUNK, dtype=jnp.float32)
    diff = idx[:, None] - idx[None, :]
    intra = jnp.where(diff >= 0, jnp.exp(jnp.maximum(diff, 0.0)[None] * log_g[:, None, None]), 0.0)
    scores = jnp.einsum('bnihd,bnjhd->bnhij', qc, kc) * intra
    o_intra = jnp.einsum('bnhij,bnjhe->bnihe', scores, vc)
    k_dec = jnp.exp((RET_CHUNK - 1 - idx)[None, :] * log_g[:, None])
    kv = jnp.einsum('bnjhd,hj,bnjhe->nbhde', kc, k_dec, vc)
    c_dec = jnp.exp(RET_CHUNK * log_g)[None, :, None, None]

    def step(s, kv_n):
        return c_dec * s + kv_n, s

    s_last, s_prev = lax.scan(step, s0, kv)
    q_dec = jnp.exp((idx + 1.0)[None, :] * log_g[:, None])
    o_cross = jnp.einsum('bnihd,hi,nbhde->bnihe', qc, q_dec, s_prev)
    return (o_intra + o_cross).reshape(b, length, h, dv), s_last


def _ret_out(o, g):
    b, length = o.shape[0], o.shape[1]
    return _rms(o).reshape(b, length, BRANCH_W).astype(g.dtype) * jax.nn.silu(g)


def retention_mixer(q, k, v, g, qc, kc, vc, gc, log_decay, tabs, with_ctx):
    log_g = -jnp.exp(log_decay.astype(jnp.float32))
    scale = HEAD_DIM ** -0.5
    q = axial_rope(_heads(q, RET_HEADS), tabs) * scale
    k = axial_rope(_heads(k, RET_HEADS), tabs)
    v = _heads(v, RET_HEADS)
    qc = _heads(qc, RET_HEADS) * scale
    kc = _heads(kc, RET_HEADS)
    vc = _heads(vc, RET_HEADS)
    s0 = jnp.zeros((q.shape[0], RET_HEADS, HEAD_DIM, HEAD_DIM), jnp.float32)
    oc_f, sc_f = retention_chunkwise(qc, kc, vc, log_g[0], s0)
    o_f, _ = retention_chunkwise(q, k, v, log_g[0], sc_f)
    oc_b, sc_b = retention_chunkwise(_flip(qc), _flip(kc), _flip(vc), log_g[1], s0)
    o_b, _ = retention_chunkwise(_flip(q), _flip(k), _flip(v), log_g[1], sc_b)
    out = _ret_out(o_f + _flip(o_b), g)
    out_c = _ret_out(oc_f + _flip(oc_b), gc) if with_ctx else None
    return out, out_c


def neighborhood_mixer(q, k, v, qc, kc, vc, qk_gain, rpb, with_ctx):
    b, length, _ = q.shape
    rows = length // GRID_W
    kh = min(NA_KH, rows)
    scale = HEAD_DIM ** -0.5
    q = _rms(_heads(q, NA_HEADS), qk_gain[0]) * scale
    k = _rms(_heads(k, NA_HEADS), qk_gain[1])
    v = _heads(v, NA_HEADS)
    qc = _rms(_heads(qc, NA_HEADS), qk_gain[0]) * scale
    kc = _rms(_heads(kc, NA_HEADS), qk_gain[1])
    vc = _heads(vc, NA_HEADS)
    qg = q.reshape(b, rows, GRID_W, NA_HEADS, HEAD_DIM)
    kg = k.reshape(b, rows, GRID_W, NA_HEADS, HEAD_DIM)
    vg = v.reshape(b, rows, GRID_W, NA_HEADS, HEAD_DIM)
    n_cb = GRID_W // NA_QCB
    cb = np.arange(n_cb)
    key_cols = np.clip(cb * NA_QCB - NA_KW // 2, 0, GRID_W - NA_KRW)[:, None] + np.arange(NA_KRW)
    q_cols = cb[:, None] * NA_QCB + np.arange(NA_QCB)
    win_start = np.clip(q_cols - NA_KW // 2, 0, GRID_W - NA_KW)
    kcol = key_cols[:, None, :]
    col_mask = (kcol >= win_start[:, :, None]) & (kcol < win_start[:, :, None] + NA_KW)
    col_idx = np.clip(kcol - q_cols[:, :, None] + NA_KW - 1, 0, 2 * NA_KW - 2)
    col_bias = rpb.astype(jnp.float32)[:, :, col_idx]
    mask = jnp.asarray(col_mask)[:, :, None, :]
    n_nb = kh * NA_KRW

    def row_fn(r):
        r0 = jnp.clip(r - kh // 2, 0, rows - kh)
        q_r = lax.dynamic_index_in_dim(qg, r, axis=1, keepdims=False).reshape(b, n_cb, NA_QCB, NA_HEADS, HEAD_DIM)
        k_r = lax.dynamic_slice_in_dim(kg, r0, kh, axis=1)[:, :, key_cols]
        v_r = lax.dynamic_slice_in_dim(vg, r0, kh, axis=1)[:, :, key_cols]
        bias = jnp.take(col_bias, r0 - r + jnp.arange(kh) + NA_KH - 1, axis=1).transpose(0, 2, 3, 1, 4)
        s_nb = jnp.einsum('bjqhd,brjkhd->bhjqrk', q_r, k_r).astype(jnp.float32) + bias
        s_nb = jnp.where(mask, s_nb, NEG).reshape(b, NA_HEADS, n_cb, NA_QCB, n_nb)
        s_cx = jnp.einsum('bjqhd,bchd->bhjqc', q_r, kc).astype(jnp.float32)
        prob = jax.nn.softmax(jnp.concatenate([s_nb, s_cx], axis=-1), axis=-1).astype(v.dtype)
        p_nb = prob[..., :n_nb].reshape(b, NA_HEADS, n_cb, NA_QCB, kh, NA_KRW)
        o = jnp.einsum('bhjqrk,brjkhd->bjqhd', p_nb, v_r) + jnp.einsum('bhjqc,bchd->bjqhd', prob[..., n_nb:], vc)
        return o.reshape(b, GRID_W, BRANCH_W)

    out = lax.map(row_fn, jnp.arange(rows))
    out = jnp.moveaxis(out, 0, 1).reshape(b, length, BRANCH_W)
    out_c = None
    if with_ctx:
        prob = jax.nn.softmax(jnp.einsum('bqhd,bkhd->bhqk', qc, kc).astype(jnp.float32), axis=-1).astype(vc.dtype)
        out_c = jnp.einsum('bhqk,bkhd->bqhd', prob, vc).reshape(qc.shape[0], qc.shape[1], BRANCH_W)
    return out, out_c


def s5_discretise(lam_re, lam_im, log_step, b_re, b_im):
    dt = jnp.exp(log_step)[:, None]
    mag = jnp.exp(lam_re * dt)
    lb_re = mag * jnp.cos(lam_im * dt)
    lb_im = mag * jnp.sin(lam_im * dt)
    den = lam_re * lam_re + lam_im * lam_im
    nr = lb_re - 1.0
    coef_re = (nr * lam_re + lb_im * lam_im) / den
    coef_im = (lb_im * lam_re - nr * lam_im) / den
    bb_re = coef_re[..., None] * b_re - coef_im[..., None] * b_im
    bb_im = coef_re[..., None] * b_im + coef_im[..., None] * b_re
    return lb_re, lb_im, bb_re, bb_im


def _cplx_combine(e1, e2):
    a1r, a1i, b1r, b1i = e1
    a2r, a2i, b2r, b2i = e2
    return (a1r * a2r - a1i * a2i, a1r * a2i + a1i * a2r,
            a2r * b1r - a2i * b1i + b2r, a2r * b1i + a2i * b1r + b2i)


def s5_scan(u, lb_re, lb_im, bb_re, bb_im, x0_re, x0_im):
    length = u.shape[1]
    bu_re = jnp.einsum('blgc,gpc->blgp', u, bb_re)
    bu_im = jnp.einsum('blgc,gpc->blgp', u, bb_im)
    bu_re = bu_re.at[:, 0].add(lb_re * x0_re - lb_im * x0_im)
    bu_im = bu_im.at[:, 0].add(lb_re * x0_im + lb_im * x0_re)
    a_re = jnp.broadcast_to(lb_re[None, None], (1, length) + lb_re.shape)
    a_im = jnp.broadcast_to(lb_im[None, None], (1, length) + lb_im.shape)
    _, _, x_re, x_im = lax.associative_scan(_cplx_combine, (a_re, a_im, bu_re, bu_im), axis=1)
    return x_re, x_im


def _s5_out(y, u, d_skip, glu_w, glu_b):
    b, length = u.shape[0], u.shape[1]
    z = y.reshape(b, length, BRANCH_W) + d_skip.astype(jnp.float32) * u.astype(jnp.float32)
    z = jax.nn.gelu(z).astype(u.dtype)
    return z * jax.nn.sigmoid(z @ glu_w + glu_b)


def s5_mixer(u, uc, lam_re, lam_im, log_step, b_re, b_im, c_re, c_im, d_skip, glu_w, glu_b, with_ctx):
    f32 = jnp.float32
    ug = u.astype(f32).reshape(u.shape[0], u.shape[1], S5_GROUPS, S5_GROUP_CH)
    ucg = uc.astype(f32).reshape(uc.shape[0], uc.shape[1], S5_GROUPS, S5_GROUP_CH)
    cr, ci = c_re.astype(f32), c_im.astype(f32)
    zero = jnp.zeros((u.shape[0], S5_GROUPS, S5_STATE), f32)
    ys, ycs = [], []
    for direction in range(2):
        disc = s5_discretise(lam_re[direction].astype(f32), lam_im[direction].astype(f32),
                             log_step[direction].astype(f32), b_re.astype(f32), b_im.astype(f32))
        orient = _flip if direction == 1 else (lambda t: t)
        xc_re, xc_im = s5_scan(orient(ucg), *disc, zero, zero)
        x_re, x_im = s5_scan(orient(ug), *disc, xc_re[:, -1], xc_im[:, -1])
        ys.append(orient(jnp.einsum('blgp,gcp->blgc', x_re, cr) - jnp.einsum('blgp,gcp->blgc', x_im, ci)))
        if with_ctx:
            ycs.append(orient(jnp.einsum('blgp,gcp->blgc', xc_re, cr) - jnp.einsum('blgp,gcp->blgc', xc_im, ci)))
    out = _s5_out(ys[0] + ys[1], u, d_skip, glu_w, glu_b)
    out_c = _s5_out(ycs[0] + ycs[1], uc, d_skip, glu_w, glu_b) if with_ctx else None
    return out, out_c


def _softmax_with_sink(s, sink_g):
    sink_col = jnp.broadcast_to(sink_g[None, :, :, None, None], s.shape[:-1] + (1,))
    return jax.nn.softmax(jnp.concatenate([s, sink_col], axis=-1), axis=-1)[..., :-1]


def window_gqa_mixer(q, k, v, qc, kc, vc, qk_gain, sink, tabs, with_ctx):
    b, length, _ = q.shape
    grp = GQA_Q_HEADS // GQA_KV_HEADS
    scale = HEAD_DIM ** -0.5
    q = axial_rope(_rms(_heads(q, GQA_Q_HEADS), qk_gain[0]), tabs) * scale
    k = axial_rope(_rms(_heads(k, GQA_KV_HEADS), qk_gain[1]), tabs)
    v = _heads(v, GQA_KV_HEADS)
    qc = _rms(_heads(qc, GQA_Q_HEADS), qk_gain[0]) * scale
    kc = _rms(_heads(kc, GQA_KV_HEADS), qk_gain[1])
    vc = _heads(vc, GQA_KV_HEADS)
    sink_g = sink.astype(jnp.float32).reshape(GQA_KV_HEADS, grp)
    qg = q.reshape(b, length, GQA_KV_HEADS, grp, HEAD_DIM)
    pad = ((0, 0), (WIN, WIN), (0, 0), (0, 0))
    k_pad, v_pad = jnp.pad(k, pad), jnp.pad(v, pad)
    kb_len = ATT_BLOCK + 2 * WIN
    rel = (jnp.arange(kb_len) - WIN)[None, :] - jnp.arange(ATT_BLOCK)[:, None]
    in_band = jnp.abs(rel) <= WIN

    def block_fn(n):
        s0 = n * ATT_BLOCK
        qb = lax.dynamic_slice_in_dim(qg, s0, ATT_BLOCK, axis=1)
        kb = lax.dynamic_slice_in_dim(k_pad, s0, kb_len, axis=1)
        vb = lax.dynamic_slice_in_dim(v_pad, s0, kb_len, axis=1)
        kpos = s0 - WIN + jnp.arange(kb_len)
        valid = in_band & ((kpos >= 0) & (kpos < length))[None, :]
        s_w = jnp.where(valid, jnp.einsum('bqkgd,bskd->bkgqs', qb, kb).astype(jnp.float32), NEG)
        s_c = jnp.einsum('bqkgd,bckd->bkgqc', qb, kc).astype(jnp.float32)
        prob = _softmax_with_sink(jnp.concatenate([s_w, s_c], axis=-1), sink_g).astype(v.dtype)
        o = (jnp.einsum('bkgqs,bskd->bqkgd', prob[..., :kb_len], vb)
             + jnp.einsum('bkgqc,bckd->bqkgd', prob[..., kb_len:], vc))
        return o.reshape(b, ATT_BLOCK, BRANCH_W)

    out = lax.map(block_fn, jnp.arange(length // ATT_BLOCK))
    out = jnp.moveaxis(out, 0, 1).reshape(b, length, BRANCH_W)
    out_c = None
    if with_ctx:
        qcg = qc.reshape(qc.shape[0], qc.shape[1], GQA_KV_HEADS, grp, HEAD_DIM)
        s = jnp.einsum('bqkgd,bckd->bkgqc', qcg, kc).astype(jnp.float32)
        prob = _softmax_with_sink(s, sink_g).astype(vc.dtype)
        out_c = jnp.einsum('bkgqc,bckd->bqkgd', prob, vc).reshape(qc.shape[0], qc.shape[1], BRANCH_W)
    return out, out_c


def _merge(ys, gates, w_branch, w_out):
    gates = gates.reshape(gates.shape[:-1] + (N_BRANCH, D_MODEL))
    merged = None
    for i in range(N_BRANCH):
        term = jax.nn.sigmoid(gates[..., i, :]) * (ys[i] @ w_branch[i])
        merged = term if merged is None else merged + term
    return merged @ w_out


def mixer_sublayer(a, ac, tabs, with_ctx, w_in, w_branch, w_out, ret_log_decay, na_qk_gain, na_rpb,
                   s5_lambda_re, s5_lambda_im, s5_log_step, s5_b_re, s5_b_im, s5_c_re, s5_c_im, s5_d,
                   s5_glu_w, s5_glu_b, gqa_qk_gain, gqa_sink):
    p = _split(a @ w_in, len(IN_SIZES))
    if with_ctx:
        pc = _split(ac @ w_in, len(IN_SIZES))
    else:
        pc = _split(ac @ w_in[:, :MIX_COLS], len(IN_SIZES) - 1)
    ya, yca = retention_mixer(p[0], p[1], p[2], p[3], pc[0], pc[1], pc[2], pc[3], ret_log_decay, tabs, with_ctx)
    yb, ycb = neighborhood_mixer(p[4], p[5], p[6], pc[4], pc[5], pc[6], na_qk_gain, na_rpb, with_ctx)
    yc, ycc = s5_mixer(p[7], pc[7], s5_lambda_re, s5_lambda_im, s5_log_step, s5_b_re, s5_b_im,
                       s5_c_re, s5_c_im, s5_d, s5_glu_w, s5_glu_b, with_ctx)
    yd, ycd = window_gqa_mixer(p[8], p[9], p[10], pc[8], pc[9], pc[10], gqa_qk_gain, gqa_sink, tabs, with_ctx)
    out = _merge((ya, yb, yc, yd), p[11], w_branch, w_out)
    out_c = _merge((yca, ycb, ycc, ycd), pc[11], w_branch, w_out) if with_ctx else None
    return out, out_c


def hier_moe(xf, rw1, rb1, rw2, rb2, w1, w3, w2):
    n, d = xf.shape
    f32 = jnp.float32
    lg1 = (xf @ rw1).astype(f32) + rb1.astype(f32)
    grp = jnp.argmax(lg1, axis=-1)
    p_grp = jnp.take_along_axis(jax.nn.softmax(lg1, axis=-1), grp[:, None], axis=-1)
    lg2 = ((xf @ rw2).astype(f32) + rb2.astype(f32)).reshape(n, MOE_GROUPS, EXPERTS_PER_GROUP)
    lg2 = jnp.take_along_axis(lg2, grp[:, None, None], axis=1)[:, 0]
    top_v, top_i = lax.top_k(lg2, TOP_K)
    gate = p_grp * jax.nn.softmax(top_v, axis=-1)
    expert = grp[:, None].astype(jnp.int32) * EXPERTS_PER_GROUP + top_i.astype(jnp.int32)
    n_assign = n * TOP_K
    e_flat = expert.reshape(-1)
    tok_flat = jnp.repeat(jnp.arange(n, dtype=jnp.int32), TOP_K)
    g_flat = gate.reshape(-1)
    order = jnp.argsort(e_flat)
    e_s, t_s, g_s = e_flat[order], tok_flat[order], g_flat[order]
    counts = jnp.bincount(e_flat, length=N_EXPERTS)
    starts = jnp.cumsum(counts) - counts
    padded = (counts + MOE_BLOCK - 1) // MOE_BLOCK * MOE_BLOCK
    pad_ends = jnp.cumsum(padded)
    pad_starts = pad_ends - padded
    dest = pad_starts[e_s] + jnp.arange(n_assign, dtype=jnp.int32) - starts[e_s]
    n_blocks = -(-(n_assign + N_EXPERTS * (MOE_BLOCK - 1)) // MOE_BLOCK)
    cap = n_blocks * MOE_BLOCK
    buf_tok = jnp.full((cap,), n, jnp.int32).at[dest].set(t_s)
    buf_gate = jnp.zeros((cap,), f32).at[dest].set(g_s)
    blk_expert = jnp.minimum(jnp.searchsorted(pad_ends, jnp.arange(n_blocks, dtype=jnp.int32) * MOE_BLOCK, side='right'),
                             N_EXPERTS - 1)
    x_pad = jnp.concatenate([xf, jnp.zeros((1, d), xf.dtype)], axis=0)
    xb = x_pad[buf_tok].reshape(n_blocks, MOE_BLOCK, d)

    def expert_block(args):
        xblk, e = args
        hid = jax.nn.silu(xblk @ w1[e]) * (xblk @ w3[e])
        return hid @ w2[e]

    yb = lax.map(expert_block, (xb, blk_expert)).reshape(cap, d)
    y = jax.ops.segment_sum(yb * buf_gate[:, None].astype(yb.dtype), buf_tok, num_segments=n + 1)
    return y[:n]


def setup_inputs(seed: int = 0) -> dict:
    key = jax.random.key(seed)
    ks = jax.random.split(key, 40)
    f32 = jnp.float32

    def nrm(k, shape, scale):
        return jax.random.normal(k, shape, f32) * scale

    ret_base = jnp.asarray(np.log(-np.log(1.0 - 2.0 ** (-5.0 - np.arange(RET_HEADS)))), f32)
    return {
        'x': nrm(ks[0], (BATCH, SEQ, D_MODEL), 1.0),
        'c': nrm(ks[1], (BATCH, D_MODEL), 1.0),
        'ctx': nrm(ks[2], (BATCH, CTX_LEN, D_MODEL), 1.0),
        'c_ctx': nrm(ks[3], (D_MODEL,), 1.0),
        'mod_w': nrm(ks[4], (DEPTH, D_MODEL, 6 * D_MODEL), 0.5 * D_MODEL ** -0.5),
        'mod_b': nrm(ks[5], (DEPTH, 6 * D_MODEL), 0.02),
        'norm1_g': 1.0 + nrm(ks[6], (DEPTH, D_MODEL), 0.02),
        'norm2_g': 1.0 + nrm(ks[7], (DEPTH, D_MODEL), 0.02),
        'w_in': nrm(ks[8], (DEPTH, D_MODEL, IN_COLS), D_MODEL ** -0.5),
        'w_branch': nrm(ks[9], (DEPTH, N_BRANCH, BRANCH_W, D_MODEL), BRANCH_W ** -0.5),
        'w_out': nrm(ks[10], (DEPTH, D_MODEL, D_MODEL), D_MODEL ** -0.5),
        'ret_log_decay': ret_base[None, None, :] + nrm(ks[11], (DEPTH, 2, RET_HEADS), 0.01),
        'na_qk_gain': 1.0 + nrm(ks[12], (DEPTH, 2, HEAD_DIM), 0.02),
        'na_rpb': nrm(ks[13], (DEPTH, NA_HEADS, 2 * NA_KH - 1, 2 * NA_KW - 1), 0.1),
        's5_lambda_re': -0.5 + nrm(ks[14], (DEPTH, 2, S5_GROUPS, S5_STATE), 0.01),
        's5_lambda_im': jnp.pi * jnp.arange(S5_STATE, dtype=f32) + nrm(ks[15], (DEPTH, 2, S5_GROUPS, S5_STATE), 0.01),
        's5_log_step': jax.random.uniform(ks[16], (DEPTH, 2, S5_GROUPS), f32, math.log(1e-3), math.log(1e-1)),
        's5_b_re': nrm(ks[17], (DEPTH, S5_GROUPS, S5_STATE, S5_GROUP_CH), (2.0 * S5_GROUP_CH) ** -0.5),
        's5_b_im': nrm(ks[18], (DEPTH, S5_GROUPS, S5_STATE, S5_GROUP_CH), (2.0 * S5_GROUP_CH) ** -0.5),
        's5_c_re': nrm(ks[19], (DEPTH, S5_GROUPS, S5_GROUP_CH, S5_STATE), S5_STATE ** -0.5),
        's5_c_im': nrm(ks[20], (DEPTH, S5_GROUPS, S5_GROUP_CH, S5_STATE), S5_STATE ** -0.5),
        's5_d': nrm(ks[21], (DEPTH, BRANCH_W), 0.5),
        's5_glu_w': nrm(ks[22], (DEPTH, BRANCH_W, BRANCH_W), BRANCH_W ** -0.5),
        's5_glu_b': nrm(ks[23], (DEPTH, BRANCH_W), 0.02),
        'gqa_qk_gain': 1.0 + nrm(ks[24], (DEPTH, 2, HEAD_DIM), 0.02),
        'gqa_sink': nrm(ks[25], (DEPTH, GQA_Q_HEADS), 0.5),
        'router_w1': nrm(ks[26], (DEPTH, D_MODEL, MOE_GROUPS), D_MODEL ** -0.5),
        'router_b1': nrm(ks[27], (DEPTH, MOE_GROUPS), 0.01),
        'router_w2': nrm(ks[28], (DEPTH, D_MODEL, N_EXPERTS), D_MODEL ** -0.5),
        'router_b2': nrm(ks[29], (DEPTH, N_EXPERTS), 0.01),
        'exp_w1': nrm(ks[30], (DEPTH, N_EXPERTS, D_MODEL, D_FF_EXPERT), D_MODEL ** -0.5),
        'exp_w3': nrm(ks[31], (DEPTH, N_EXPERTS, D_MODEL, D_FF_EXPERT), D_MODEL ** -0.5),
        'exp_w2': nrm(ks[32], (DEPTH, N_EXPERTS, D_FF_EXPERT, D_MODEL), D_FF_EXPERT ** -0.5),
    }


def reference(x, c, ctx, c_ctx, mod_w, mod_b, norm1_g, norm2_g, w_in, w_branch, w_out, ret_log_decay,
              na_qk_gain, na_rpb, s5_lambda_re, s5_lambda_im, s5_log_step, s5_b_re, s5_b_im, s5_c_re,
              s5_c_im, s5_d, s5_glu_w, s5_glu_b, gqa_qk_gain, gqa_sink, router_w1, router_b1, router_w2,
              router_b2, exp_w1, exp_w3, exp_w2):
    tabs = rope_tables(x.shape[1])
    h, hc = x, ctx
    sc, scc = jax.nn.silu(c), jax.nn.silu(c_ctx)
    for i in range(DEPTH):
        with_ctx = i < DEPTH - 1
        mod = jnp.split((sc @ mod_w[i] + mod_b[i])[:, None, :], 6, axis=-1)
        modc = jnp.split(scc @ mod_w[i] + mod_b[i], 6)
        a = _modulate(_rms(h, norm1_g[i]), mod[0], mod[1])
        ac = _modulate(_rms(hc, norm1_g[i]), modc[0], modc[1])
        y, yc = mixer_sublayer(a, ac, tabs, with_ctx, w_in[i], w_branch[i], w_out[i], ret_log_decay[i],
                               na_qk_gain[i], na_rpb[i], s5_lambda_re[i], s5_lambda_im[i], s5_log_step[i],
                               s5_b_re[i], s5_b_im[i], s5_c_re[i], s5_c_im[i], s5_d[i], s5_glu_w[i],
                               s5_glu_b[i], gqa_qk_gain[i], gqa_sink[i])
        h = h + mod[2] * y
        if with_ctx:
            hc = hc + modc[2] * yc
        f = _modulate(_rms(h, norm2_g[i]), mod[3], mod[4])
        if with_ctx:
            fc = _modulate(_rms(hc, norm2_g[i]), modc[3], modc[4])
            n_ctx = fc.shape[0] * fc.shape[1]
            tokens = jnp.concatenate([fc.reshape(n_ctx, D_MODEL), f.reshape(-1, D_MODEL)], axis=0)
            out = hier_moe(tokens, router_w1[i], router_b1[i], router_w2[i], router_b2[i],
                           exp_w1[i], exp_w3[i], exp_w2[i])
            hc = hc + modc[5] * out[:n_ctx].reshape(hc.shape)
            h = h + mod[5] * out[n_ctx:].reshape(h.shape)
        else:
            out = hier_moe(f.reshape(-1, D_MODEL), router_w1[i], router_b1[i], router_w2[i], router_b2[i],
                           exp_w1[i], exp_w3[i], exp_w2[i])
            h = h + mod[5] * out.reshape(h.shape)
    return h
```

```python
import functools
import math

import numpy as np
import jax
import jax.numpy as jnp
from jax import lax
from jax.experimental import pallas as pl
from jax.experimental.pallas import tpu as pltpu

F32 = jnp.float32
BF16 = jnp.bfloat16

D_MODEL = 1024
GRID_W = 64
HEAD_DIM = 64
N_BRANCH = 4
BRANCH_W = D_MODEL // 4
N_HEADS = BRANCH_W // HEAD_DIM
RET_CHUNK = 128
NA_KH = 8
NA_KW = 16
NA_QROWS = 4
NA_KROWS = NA_QROWS + NA_KH
S5_GROUP_CH = 16
S5_GROUPS = BRANCH_W // S5_GROUP_CH
S5_STATE = 64
S5_CHUNK = 16
GQA_KV_HEADS = 2
GQA_KV_W = GQA_KV_HEADS * HEAD_DIM
WIN = 128
MOE_GROUPS = 4
EXPERTS_PER_GROUP = 8
N_EXPERTS = MOE_GROUPS * EXPERTS_PER_GROUP
D_FF_EXPERT = D_MODEL // 2
ROPE_BASE = 10000.0
EPS = 1e-6
NEG = -1e30
MIX_COLS = 9 * BRANCH_W + 2 * GQA_KV_W

ROW_TILE = 256
MOE_ROWS = 256
LOGIT_W = 128
VMEM_LIMIT = 56 * 1024 * 1024


def _const_spec(shape):
    nd = len(shape)
    return pl.BlockSpec(shape, lambda *_: (0,) * nd)


def _dot(a, b):
    return jnp.dot(a, b, preferred_element_type=F32)


def _dot_nt(a, b):
    return lax.dot_general(a, b, (((1,), (1,)), ((), ())), preferred_element_type=F32)


def _dot_tn(a, b):
    return lax.dot_general(a, b, (((0,), (0,)), ((), ())), preferred_element_type=F32)


def _split_bf16(x):
    hi = x.astype(BF16)
    lo = (x - hi.astype(F32)).astype(BF16)
    return hi, lo


def _norm_modulate(x, gain, mod, shift_row, scale_row):
    ms = jnp.mean(x * x, axis=-1, keepdims=True)
    a = x * lax.rsqrt(ms + EPS) * gain
    return a * (1.0 + mod[scale_row:scale_row + 1]) + mod[shift_row:shift_row + 1]


def _mod_kernel(c_ref, w_ref, b_ref, o_ref):
    c = c_ref[...]
    s = c * jax.nn.sigmoid(c)
    o_ref[0] = _dot(s.astype(BF16), w_ref[0].astype(BF16)) + b_ref[0]


def _modulation(c, c_ctx, mod_w, mod_b):
    depth = mod_w.shape[0]
    b = c.shape[0]
    rows = -(-(b + 1) // 8) * 8
    cs = jnp.zeros((rows, D_MODEL), F32).at[:b].set(c).at[b].set(c_ctx)
    out = pl.pallas_call(
        _mod_kernel,
        out_shape=jax.ShapeDtypeStruct((depth, rows, 6 * D_MODEL), F32),
        grid=(depth, 6),
        in_specs=[
            _const_spec((rows, D_MODEL)),
            pl.BlockSpec((1, D_MODEL, D_MODEL), lambda l, j: (l, 0, j)),
            pl.BlockSpec((1, 1, D_MODEL), lambda l, j: (l, 0, j)),
        ],
        out_specs=pl.BlockSpec((1, rows, D_MODEL), lambda l, j: (l, 0, j)),
        compiler_params=pltpu.CompilerParams(dimension_semantics=("parallel", "parallel")),
        name="adaln_mod",
    )(cs, mod_w, mod_b.reshape(depth, 1, 6 * D_MODEL))
    lat = out[:, :b].reshape(depth, b, 6, D_MODEL)
    ctx = jnp.broadcast_to(out[:, b].reshape(depth, 1, 6, D_MODEL), (depth, b, 6, D_MODEL))
    modall = jnp.stack([ctx, lat], axis=2)
    return jnp.pad(modall, ((0, 0), (0, 0), (0, 0), (0, 2), (0, 0)))


def _head_mean_sq(x, ones_bd):
    hi, lo = _split_bf16(x * x)
    return _dot(hi, ones_bd) + _dot(lo, ones_bd)


def _rope(x, cos, sin_up, sin_dn):
    w = x.shape[-1]
    return x * cos + pltpu.roll(x, w - 16, 1) * sin_up + pltpu.roll(x, 16, 1) * sin_dn


def _inproj_kernel(h_ref, g1_ref, mod_ref, w_ref, cos_ref, sup_ref, sdn_ref, gains_ref, ones_ref,
                   ret_ref, na_ref, u_ref, gqa_ref):
    a = _norm_modulate(h_ref[0], g1_ref[...], mod_ref[0, 0], 0, 1)
    p = _dot(a.astype(BF16), w_ref[...])
    cos, sup, sdn = cos_ref[...], sup_ref[...], sdn_ref[...]
    ones_bd = ones_ref[...]
    scale = HEAD_DIM ** -0.5
    bw = BRANCH_W

    def col(i):
        return p[:, i * bw:(i + 1) * bw]

    def head_rms(x, gain):
        w = x.shape[-1]
        ms = _head_mean_sq(x, ones_bd[:w, :w])
        return x * lax.rsqrt(ms + EPS) * gain

    ret_ref[0, :, 0 * bw:1 * bw] = (_rope(col(0), cos, sup, sdn) * scale).astype(BF16)
    ret_ref[0, :, 1 * bw:2 * bw] = _rope(col(1), cos, sup, sdn).astype(BF16)
    ret_ref[0, :, 2 * bw:3 * bw] = col(2).astype(BF16)
    ret_ref[0, :, 3 * bw:4 * bw] = col(3).astype(BF16)
    na_ref[0, :, 0 * bw:1 * bw] = (head_rms(col(4), gains_ref[0:1]) * scale).astype(BF16)
    na_ref[0, :, 1 * bw:2 * bw] = head_rms(col(5), gains_ref[1:2]).astype(BF16)
    na_ref[0, :, 2 * bw:3 * bw] = col(6).astype(BF16)
    u_ref[0] = col(7)
    gq = _rope(head_rms(col(8), gains_ref[2:3]), cos, sup, sdn) * scale
    gqa_ref[0, :, 0:bw] = gq.astype(BF16)
    kw = GQA_KV_W
    gk = head_rms(p[:, 9 * bw:9 * bw + kw], gains_ref[3:4, :kw])
    gk = _rope(gk, cos[:, :kw], sup[:, :kw], sdn[:, :kw])
    gqa_ref[0, :, bw:bw + kw] = gk.astype(BF16)
    gqa_ref[0, :, bw + kw:bw + 2 * kw] = p[:, 9 * bw + kw:9 * bw + 2 * kw].astype(BF16)


def _mod_spec(nct):
    return pl.BlockSpec((1, 1, 8, D_MODEL), lambda b, t: (b, jnp.where(t >= nct, 1, 0), 0, 0))


def _inproj(hall, g1, modall, w_mix, rope_tabs, gains, ones_bd, nct):
    b, t, _ = hall.shape
    tl = ROW_TILE
    row = lambda w: pl.BlockSpec((1, tl, w), lambda b, t: (b, t, 0))
    tab = pl.BlockSpec((tl, BRANCH_W), lambda b, t: (t, 0))
    return pl.pallas_call(
        _inproj_kernel,
        out_shape=(
            jax.ShapeDtypeStruct((b, t, 4 * BRANCH_W), BF16),
            jax.ShapeDtypeStruct((b, t, 3 * BRANCH_W), BF16),
            jax.ShapeDtypeStruct((b, t, BRANCH_W), F32),
            jax.ShapeDtypeStruct((b, t, BRANCH_W + 2 * GQA_KV_W), BF16),
        ),
        grid=(b, t // tl),
        in_specs=[
            row(D_MODEL), _const_spec((1, D_MODEL)), _mod_spec(nct), _const_spec((D_MODEL, MIX_COLS)),
            tab, tab, tab, _const_spec((4, BRANCH_W)), _const_spec((BRANCH_W, BRANCH_W)),
        ],
        out_specs=(row(4 * BRANCH_W), row(3 * BRANCH_W), row(BRANCH_W), row(BRANCH_W + 2 * GQA_KV_W)),
        compiler_params=pltpu.CompilerParams(dimension_semantics=("parallel", "parallel"),
                                             vmem_limit_bytes=VMEM_LIMIT),
        name="inproj",
    )(hall, g1, modall, w_mix, *rope_tabs, gains, ones_bd)


def _ret_consts(log_decay):
    lg = -jnp.exp(log_decay.astype(F32))
    idx = jnp.arange(RET_CHUNK, dtype=F32)
    diff = idx[:, None] - idx[None, :]
    fwd = jnp.where(diff > 0, jnp.exp(jnp.maximum(diff, 0.0)[None] * lg[0][:, None, None]), 0.0)
    bwd = jnp.where(diff < 0, jnp.exp(jnp.maximum(-diff, 0.0)[None] * lg[1][:, None, None]), 0.0)
    dmask = fwd + bwd + 2.0 * jnp.eye(RET_CHUNK, dtype=F32)[None]
    tabs = jnp.stack([
        jnp.exp((RET_CHUNK - 1 - idx)[None, :] * lg[0][:, None]),
        jnp.exp(idx[None, :] * lg[1][:, None]),
        jnp.exp((idx + 1.0)[None, :] * lg[0][:, None]),
        jnp.exp((RET_CHUNK - idx)[None, :] * lg[1][:, None]),
    ])
    tabs = jnp.repeat(tabs.transpose(0, 2, 1), HEAD_DIM, axis=-1)
    cdec = jnp.exp(RET_CHUNK * lg).reshape(2 * N_HEADS)
    return dmask, tabs, cdec


def _ret_bwd_chunk(i, ncc, nch):
    return jnp.where(i < ncc, ncc - 1 - i, nch - 1 - (i - ncc))


def _ret_kernel(cdec_ref, q_ref, k_ref, v_ref, g_ref, dmask_ref, tab_ref, o_ref, sf, sb, sbprev, *, ncc):
    ph = pl.program_id(1)
    i = pl.program_id(2)
    nch = pl.num_programs(2)
    hd = HEAD_DIM

    @pl.when(jnp.logical_and(ph == 0, i == 0))
    def _():
        sb[...] = jnp.zeros_like(sb)

    @pl.when(jnp.logical_and(ph == 1, i == 0))
    def _():
        sf[...] = jnp.zeros_like(sf)

    k = k_ref[0]
    v = v_ref[0]

    @pl.when(ph == 0)
    def _():
        n = _ret_bwd_chunk(i, ncc, nch)
        kd = (k.astype(F32) * tab_ref[1]).astype(BF16)
        kv = _dot_tn(kd, v)
        for h in range(N_HEADS):
            sbprev[n, h] = sb[h]
            sb[h] = cdec_ref[N_HEADS + h] * sb[h] + kv[h * hd:(h + 1) * hd, h * hd:(h + 1) * hd]

    @pl.when(ph == 1)
    def _():
        q = q_ref[0]
        qf32 = q.astype(F32)
        qf = (qf32 * tab_ref[2]).astype(BF16)
        qb = (qf32 * tab_ref[3]).astype(BF16)
        kd = (k.astype(F32) * tab_ref[0]).astype(BF16)
        kv = _dot_tn(kd, v)
        outs = []
        for h in range(N_HEADS):
            hs = slice(h * hd, (h + 1) * hd)
            s = _dot_nt(q[:, hs], k[:, hs]) * dmask_ref[h]
            o = _dot(s.astype(BF16), v[:, hs])
            o = o + _dot(qf[:, hs], sf[h].astype(BF16)) + _dot(qb[:, hs], sbprev[i, h].astype(BF16))
            sf[h] = cdec_ref[h] * sf[h] + kv[hs, hs]
            outs.append(o * lax.rsqrt(jnp.mean(o * o, axis=-1, keepdims=True) + EPS))
        g = g_ref[0].astype(F32)
        o_ref[0] = (jnp.concatenate(outs, axis=-1) * (g * jax.nn.sigmoid(g))).astype(BF16)


def _retention(ret, dmask, tabs, cdec, ncc):
    b, t, _ = ret.shape
    nch = t // RET_CHUNK
    bw = BRANCH_W

    def kv_map(col):
        return lambda b, ph, i, *_: (b, jnp.where(ph == 0, _ret_bwd_chunk(i, ncc, nch), i), col)

    def q_map(col):
        return lambda b, ph, i, *_: (b, jnp.where(ph == 0, 0, i), col)

    blk = (1, RET_CHUNK, bw)
    return pl.pallas_call(
        functools.partial(_ret_kernel, ncc=ncc),
        out_shape=jax.ShapeDtypeStruct((b, t, bw), BF16),
        grid_spec=pltpu.PrefetchScalarGridSpec(
            num_scalar_prefetch=1,
            grid=(b, 2, nch),
            in_specs=[
                pl.BlockSpec(blk, q_map(0)), pl.BlockSpec(blk, kv_map(1)), pl.BlockSpec(blk, kv_map(2)),
                pl.BlockSpec(blk, q_map(3)),
                _const_spec((N_HEADS, RET_CHUNK, RET_CHUNK)), _const_spec((4, RET_CHUNK, bw)),
            ],
            out_specs=pl.BlockSpec(blk, q_map(0)),
            scratch_shapes=[
                pltpu.VMEM((N_HEADS, HEAD_DIM, HEAD_DIM), F32),
                pltpu.VMEM((N_HEADS, HEAD_DIM, HEAD_DIM), F32),
                pltpu.VMEM((nch, N_HEADS, HEAD_DIM, HEAD_DIM), F32),
            ],
        ),
        compiler_params=pltpu.CompilerParams(dimension_semantics=("parallel", "arbitrary", "arbitrary"),
                                             vmem_limit_bytes=VMEM_LIMIT),
        name="retention",
    )(cdec, ret, ret, ret, ret, dmask, tabs)


def _na_bias_tables(rpb, rows):
    nrb = rows // NA_QROWS
    qi = np.arange(NA_QROWS)
    kl = np.arange(NA_KROWS)
    ridx, rvalid = [], []
    for rb in (0, 1, nrb - 1):
        u0 = int(np.clip(NA_QROWS * rb - NA_KH // 2, 0, rows - NA_KROWS))
        r = NA_QROWS * rb + qi[:, None]
        kr = u0 + kl[None, :]
        r0 = np.clip(r - NA_KH // 2, 0, rows - NA_KH)
        rvalid.append((kr >= r0) & (kr < r0 + NA_KH))
        ridx.append(np.clip(kr - r + NA_KH - 1, 0, 2 * NA_KH - 2))
    ridx, rvalid = np.stack(ridx), np.stack(rvalid)
    qc = np.arange(GRID_W)[:, None]
    kc = np.arange(GRID_W)[None, :]
    ws = np.clip(qc - NA_KW // 2, 0, GRID_W - NA_KW)
    cvalid = (kc >= ws) & (kc < ws + NA_KW)
    cidx = np.clip(kc - qc + NA_KW - 1, 0, 2 * NA_KW - 2)
    bias = rpb.astype(F32)[:, ridx[:, :, None, :, None], cidx[None, None, :, None, :]]
    valid = rvalid[:, :, None, :, None] & cvalid[None, None, :, None, :]
    bias = jnp.where(jnp.asarray(valid)[None], bias, NEG)
    return bias.transpose(1, 0, 2, 3, 4, 5).reshape(3, N_HEADS, NA_QROWS * GRID_W, NA_KROWS * GRID_W)


def _softmax_pv(parts, sink=None):
    m = functools.reduce(jnp.maximum, [jnp.max(s, axis=-1, keepdims=True) for s, _ in parts])
    if sink is not None:
        m = jnp.maximum(m, sink)
    den = None
    acc = None
    for s, v in parts:
        p = jnp.exp(s - m)
        d = jnp.sum(p, axis=-1, keepdims=True)
        a = _dot(p.astype(BF16), v)
        den = d if den is None else den + d
        acc = a if acc is None else acc + a
    if sink is not None:
        den = den + jnp.exp(sink - m)
    return acc / den


def _na_kernel(q_ref, k_ref, v_ref, bias_ref, o_ref, *, nct, ctx_len, rows):
    j = pl.program_id(1)
    hd = HEAD_DIM
    q = q_ref[0]
    kc = k_ref[0, 0:ctx_len, :]
    vc = v_ref[0, 0:ctx_len, :]

    @pl.when(j < nct)
    def _():
        outs = []
        for h in range(N_HEADS):
            hs = slice(h * hd, (h + 1) * hd)
            outs.append(_softmax_pv([(_dot_nt(q[:, hs], kc[:, hs]), vc[:, hs])]))
        o_ref[0] = jnp.concatenate(outs, axis=-1).astype(BF16)

    @pl.when(j >= nct)
    def _():
        rb = j - nct
        u0 = jnp.clip(NA_QROWS * rb - NA_KH // 2, 0, rows - NA_KROWS)
        start = pl.multiple_of(ctx_len + u0 * GRID_W, GRID_W)
        kw = k_ref[0, pl.ds(start, NA_KROWS * GRID_W), :]
        vw = v_ref[0, pl.ds(start, NA_KROWS * GRID_W), :]
        outs = []
        for h in range(N_HEADS):
            hs = slice(h * hd, (h + 1) * hd)
            s_nb = _dot_nt(q[:, hs], kw[:, hs]) + bias_ref[0, h]
            s_cx = _dot_nt(q[:, hs], kc[:, hs])
            outs.append(_softmax_pv([(s_nb, vw[:, hs]), (s_cx, vc[:, hs])]))
        o_ref[0] = jnp.concatenate(outs, axis=-1).astype(BF16)


def _neighborhood(na, bias, ctx_len):
    b, t, _ = na.shape
    bw = BRANCH_W
    qt = NA_QROWS * GRID_W
    nct = ctx_len // qt
    nsteps = t // qt
    nrb = nsteps - nct
    rows = nrb * NA_QROWS

    def bias_map(b, j):
        rb = j - nct
        return (jnp.where(rb <= 0, 0, jnp.where(rb == nrb - 1, 2, 1)), 0, 0, 0)

    return pl.pallas_call(
        functools.partial(_na_kernel, nct=nct, ctx_len=ctx_len, rows=rows),
        out_shape=jax.ShapeDtypeStruct((b, t, bw), BF16),
        grid=(b, nsteps),
        in_specs=[
            pl.BlockSpec((1, qt, bw), lambda b, j: (b, j, 0)),
            pl.BlockSpec((1, t, bw), lambda b, j: (b, 0, 1)),
            pl.BlockSpec((1, t, bw), lambda b, j: (b, 0, 2)),
            pl.BlockSpec((1, N_HEADS, qt, NA_KROWS * GRID_W), bias_map),
        ],
        out_specs=pl.BlockSpec((1, qt, bw), lambda b, j: (b, j, 0)),
        compiler_params=pltpu.CompilerParams(dimension_semantics=("parallel", "arbitrary"),
                                             vmem_limit_bytes=VMEM_LIMIT),
        name="neighborhood_attn",
    )(na, na, na, bias)


GQA_QT = 256
GQA_KT = GQA_QT + 2 * WIN


def _gqa_kernel(sink_ref, q_ref, kv_ref, o_ref, *, nct, ctx_len, lat_len):
    j = pl.program_id(1)
    hd = HEAD_DIM
    grp = N_HEADS // GQA_KV_HEADS
    q = q_ref[0]
    kvc = kv_ref[0, 0:ctx_len, :]

    def stacked_q(kvh):
        return jnp.concatenate([q[:, (kvh * grp + g) * hd:(kvh * grp + g + 1) * hd] for g in range(grp)], axis=0)

    def sink_col(kvh):
        return jnp.concatenate([jnp.full((GQA_QT, 1), sink_ref[kvh * grp + g], F32) for g in range(grp)], axis=0)

    def emit(results):
        cols = [results[kvh][g * GQA_QT:(g + 1) * GQA_QT] for kvh in range(GQA_KV_HEADS) for g in range(grp)]
        o_ref[0] = jnp.concatenate(cols, axis=-1).astype(BF16)

    @pl.when(j < nct)
    def _():
        res = []
        for kvh in range(GQA_KV_HEADS):
            kc = kvc[:, kvh * hd:(kvh + 1) * hd]
            vc = kvc[:, GQA_KV_W + kvh * hd:GQA_KV_W + (kvh + 1) * hd]
            res.append(_softmax_pv([(_dot_nt(stacked_q(kvh), kc), vc)], sink=sink_col(kvh)))
        emit(res)

    @pl.when(j >= nct)
    def _():
        q0 = (j - nct) * GQA_QT
        k0 = jnp.clip(q0 - WIN, 0, lat_len - GQA_KT)
        kvw = kv_ref[0, pl.ds(pl.multiple_of(ctx_len + k0, WIN), GQA_KT), :]
        qpos = q0 + (lax.broadcasted_iota(jnp.int32, (grp * GQA_QT, GQA_KT), 0) & (GQA_QT - 1))
        kpos = k0 + lax.broadcasted_iota(jnp.int32, (grp * GQA_QT, GQA_KT), 1)
        band = jnp.abs(qpos - kpos) <= WIN
        res = []
        for kvh in range(GQA_KV_HEADS):
            ks = slice(kvh * hd, (kvh + 1) * hd)
            vs = slice(GQA_KV_W + kvh * hd, GQA_KV_W + (kvh + 1) * hd)
            q2 = stacked_q(kvh)
            s_w = jnp.where(band, _dot_nt(q2, kvw[:, ks]), NEG)
            s_c = _dot_nt(q2, kvc[:, ks])
            res.append(_softmax_pv([(s_w, kvw[:, vs]), (s_c, kvc[:, vs])], sink=sink_col(kvh)))
        emit(res)


def _window_gqa(gqa, sink, ctx_len):
    b, t, _ = gqa.shape
    bw = BRANCH_W
    nct = ctx_len // GQA_QT
    return pl.pallas_call(
        functools.partial(_gqa_kernel, nct=nct, ctx_len=ctx_len, lat_len=t - ctx_len),
        out_shape=jax.ShapeDtypeStruct((b, t, bw), BF16),
        grid_spec=pltpu.PrefetchScalarGridSpec(
            num_scalar_prefetch=1,
            grid=(b, t // GQA_QT),
            in_specs=[
                pl.BlockSpec((1, GQA_QT, bw), lambda b, j, *_: (b, j, 0)),
                pl.BlockSpec((1, t, bw), lambda b, j, *_: (b, 0, 1)),
            ],
            out_specs=pl.BlockSpec((1, GQA_QT, bw), lambda b, j, *_: (b, j, 0)),
        ),
        compiler_params=pltpu.CompilerParams(dimension_semantics=("parallel", "arbitrary"),
                                             vmem_limit_bytes=VMEM_LIMIT),
        name="window_gqa",
    )(sink.astype(F32), gqa, gqa)


def _cmul(ar, ai, br, bi):
    return ar * br - ai * bi, ar * bi + ai * br


def _s5_consts(lam_re, lam_im, log_step, b_re, b_im, c_re, c_im):
    f32 = F32
    tc = S5_CHUNK
    cre, cim = c_re.astype(f32), c_im.astype(f32)
    bre, bim = b_re.astype(f32), b_im.astype(f32)
    kern, inj, cpow, scan = [], [], [], []
    for d in range(2):
        lre, lim, dt = lam_re[d].astype(f32), lam_im[d].astype(f32), jnp.exp(log_step[d].astype(f32))[:, None]
        mag = jnp.exp(lre * dt)
        ar, ai = mag * jnp.cos(lim * dt), mag * jnp.sin(lim * dt)
        den = lre * lre + lim * lim
        nr = ar - 1.0
        coef_re = (nr * lre + ai * lim) / den
        coef_im = (ai * lre - nr * lim) / den
        bbr = coef_re[..., None] * bre - coef_im[..., None] * bim
        bbi = coef_re[..., None] * bim + coef_im[..., None] * bre
        pr, pi = [jnp.ones_like(ar)], [jnp.zeros_like(ar)]
        for _ in range(tc):
            nr_, ni_ = _cmul(pr[-1], pi[-1], ar, ai)
            pr.append(nr_)
            pi.append(ni_)
        pr, pi = jnp.stack(pr), jnp.stack(pi)
        lbr, lbi = _cmul(pr[:tc, :, :, None], pi[:tc, :, :, None], bbr[None], bbi[None])
        kern.append(jnp.einsum('gcp,tgpd->tgcd', cre, lbr) - jnp.einsum('gcp,tgpd->tgcd', cim, lbi))
        order = jnp.arange(tc - 1, -1, -1) if d == 0 else jnp.arange(tc)
        inj.append((lbr[order], lbi[order]))
        pw = jnp.arange(1, tc + 1) if d == 0 else jnp.arange(tc, 0, -1)
        dr, di = _cmul(cre[None], cim[None], pr[pw][:, :, None, :], pi[pw][:, :, None, :])
        cpow.append((dr, -di))
        scan.append((pr[tc], pi[tc]))
    g, gc, p = S5_GROUPS, S5_GROUP_CH, S5_STATE
    lag = jnp.arange(tc)[None, :] - jnp.arange(tc)[:, None]
    kf = kern[0][jnp.clip(lag, 0, tc - 1)]
    kb = kern[1][jnp.clip(-lag, 0, tc - 1)]
    sel = lambda cond: cond[:, :, None, None, None]
    toep = jnp.where(sel(lag > 0), kf, 0.0) + jnp.where(sel(lag < 0), kb, 0.0) + jnp.where(sel(lag == 0), kf + kb, 0.0)
    toep = toep.transpose(2, 0, 4, 1, 3).reshape(g, tc * gc, tc * gc)
    injm = jnp.concatenate([x.transpose(1, 0, 3, 2).reshape(g, tc * gc, p)
                            for x in (inj[0][0], inj[0][1], inj[1][0], inj[1][1])], axis=-1)
    w = jnp.concatenate([toep, injm], axis=-1)
    cp = jnp.concatenate([x.transpose(1, 3, 0, 2).reshape(g, p, tc * gc)
                          for x in (cpow[0][0], cpow[0][1], cpow[1][0], cpow[1][1])], axis=1)
    rows = []
    for d in range(2):
        sr, si = scan[d]
        rows += [jnp.concatenate([sr, sr], -1), jnp.concatenate([-si, si], -1)]
    scan_c = jnp.stack(rows, axis=1)
    scan_c = jnp.pad(scan_c, ((0, 0), (0, 4), (0, 0)))
    return w.astype(BF16), cp.astype(BF16), scan_c


def _s5_kernel(u_ref, w_ref, cp_ref, sc_ref, y_ref, inj, injs, xs, *, bsz, nch, ncc):
    p2 = 2 * S5_STATE
    r = _dot(u_ref[0], w_ref[0])
    y_ref[0] = r[:, :BRANCH_W]
    rf = r[:, BRANCH_W:BRANCH_W + p2]
    rb = r[:, BRANCH_W + p2:]
    inj[:, 0:p2] = rf
    inj[:, p2:] = rb
    injs[:, 0:p2] = pltpu.roll(rf, S5_STATE, 1)
    injs[:, p2:] = pltpu.roll(rb, S5_STATE, 1)
    a_f, s_f, a_b, s_b = sc_ref[0, 0:1], sc_ref[0, 1:2], sc_ref[0, 2:3], sc_ref[0, 3:4]

    def body(n, carry):
        xf, xfs, xb, xbs = carry
        rowf = pl.multiple_of(n * bsz, bsz)
        nb = jnp.where(n < ncc, ncc - 1 - n, nch - 1 - (n - ncc))
        rowb = pl.multiple_of(nb * bsz, bsz)
        xs[pl.ds(rowf, bsz), 0:p2] = xf
        xs[pl.ds(rowb, bsz), p2:] = xb
        nxf = a_f * xf + s_f * xfs + inj[pl.ds(rowf, bsz), 0:p2]
        nxfs = a_f * xfs - s_f * xf + injs[pl.ds(rowf, bsz), 0:p2]
        nxb = a_b * xb + s_b * xbs + inj[pl.ds(rowb, bsz), p2:]
        nxbs = a_b * xbs - s_b * xb + injs[pl.ds(rowb, bsz), p2:]
        return nxf, nxfs, nxb, nxbs

    z = jnp.zeros((bsz, p2), F32)
    lax.fori_loop(0, nch, body, (z, z, z, z))
    y_ref[0] += _dot(xs[...].astype(BF16), cp_ref[0])


def _s5_core(u, w, cp, scan_c, ctx_len):
    b, t, bw = u.shape
    g, gc, tc = S5_GROUPS, S5_GROUP_CH, S5_CHUNK
    nch = t // tc
    rows = nch * b
    ug = u.reshape(b, nch, tc, g, gc).transpose(3, 1, 0, 2, 4).reshape(g, rows, tc * gc).astype(BF16)
    y = pl.pallas_call(
        functools.partial(_s5_kernel, bsz=b, nch=nch, ncc=ctx_len // tc),
        out_shape=jax.ShapeDtypeStruct((g, rows, bw), F32),
        grid=(g,),
        in_specs=[
            pl.BlockSpec((1, rows, bw), lambda i: (i, 0, 0)),
            pl.BlockSpec((1, bw, bw + 4 * S5_STATE), lambda i: (i, 0, 0)),
            pl.BlockSpec((1, 4 * S5_STATE, bw), lambda i: (i, 0, 0)),
            pl.BlockSpec((1, 8, 2 * S5_STATE), lambda i: (i, 0, 0)),
        ],
        out_specs=pl.BlockSpec((1, rows, bw), lambda i: (i, 0, 0)),
        scratch_shapes=[pltpu.VMEM((rows, 4 * S5_STATE), F32)] * 3,
        compiler_params=pltpu.CompilerParams(dimension_semantics=("parallel",), vmem_limit_bytes=VMEM_LIMIT),
        name="s5_chunked",
    )(ug, w, cp, scan_c)
    return y.reshape(g, nch, b, tc, gc).transpose(2, 1, 3, 0, 4).reshape(b, t, bw)


def _merge_kernel(h_ref, g1_ref, mod_ref, ya_ref, yb_ref, ys_ref, u_ref, yd_ref, s5p_ref, gluw_ref,
                  wg_ref, wb_ref, wo_ref, g2_ref, rwh_ref, rwl_ref, rb_ref,
                  hout_ref, f_ref, lg_ref):
    x = h_ref[0]
    mod = mod_ref[0, 0]
    a = _norm_modulate(x, g1_ref[...], mod, 0, 1)
    gates = _dot(a.astype(BF16), wg_ref[...])
    z = ys_ref[0] + s5p_ref[0:1] * u_ref[0]
    z = jax.nn.gelu(z)
    yc = z * jax.nn.sigmoid(_dot(z.astype(BF16), gluw_ref[...]) + s5p_ref[1:2])
    ys = (ya_ref[0], yb_ref[0], yc.astype(BF16), yd_ref[0])
    merged = None
    for i in range(N_BRANCH):
        term = jax.nn.sigmoid(gates[:, i * D_MODEL:(i + 1) * D_MODEL]) * _dot(ys[i], wb_ref[i])
        merged = term if merged is None else merged + term
    hn = x + mod[2:3] * _dot(merged.astype(BF16), wo_ref[...])
    hout_ref[0] = hn
    f = _norm_modulate(hn, g2_ref[...], mod, 3, 4)
    f_ref[0] = f
    fh, fl = _split_bf16(f)
    lg_ref[0] = _dot(fh, rwh_ref[...]) + _dot(fl, rwh_ref[...]) + _dot(fh, rwl_ref[...]) + rb_ref[...]


def _merge(hall, g1, modall, ya, yb, ys, u, yd, s5p, gluw, wg, wb, wo, g2, rwh, rwl, rbias, nct):
    b, t, _ = hall.shape
    tl = ROW_TILE
    row = lambda w: pl.BlockSpec((1, tl, w), lambda b, t: (b, t, 0))
    return pl.pallas_call(
        _merge_kernel,
        out_shape=(
            jax.ShapeDtypeStruct((b, t, D_MODEL), F32),
            jax.ShapeDtypeStruct((b, t, D_MODEL), F32),
            jax.ShapeDtypeStruct((b, t, LOGIT_W), F32),
        ),
        grid=(b, t // tl),
        in_specs=[
            row(D_MODEL), _const_spec((1, D_MODEL)), _mod_spec(nct),
            row(BRANCH_W), row(BRANCH_W), row(BRANCH_W), row(BRANCH_W), row(BRANCH_W),
            _const_spec((8, BRANCH_W)), _const_spec((BRANCH_W, BRANCH_W)),
            _const_spec((D_MODEL, N_BRANCH * D_MODEL)), _const_spec((N_BRANCH, BRANCH_W, D_MODEL)),
            _const_spec((D_MODEL, D_MODEL)), _const_spec((1, D_MODEL)),
            _const_spec((D_MODEL, LOGIT_W)), _const_spec((D_MODEL, LOGIT_W)), _const_spec((1, LOGIT_W)),
        ],
        out_specs=(row(D_MODEL), row(D_MODEL), row(LOGIT_W)),
        compiler_params=pltpu.CompilerParams(dimension_semantics=("parallel", "parallel"),
                                             vmem_limit_bytes=VMEM_LIMIT),
        name="merge_router",
    )(hall, g1, modall, ya, yb, ys, u, yd, s5p, gluw, wg, wb, wo, g2, rwh, rwl, rbias)


def _route_kernel(lg_ref, idx_ref, gate_ref):
    x = lg_ref[...]
    lane = lax.broadcasted_iota(jnp.int32, x.shape, 1).astype(F32)
    big = 1e9
    ninf = -jnp.inf
    in_g = lane < MOE_GROUPS
    m1 = jnp.max(jnp.where(in_g, x, ninf), axis=-1, keepdims=True)
    grp = jnp.min(jnp.where(in_g & (x == m1), lane, big), axis=-1, keepdims=True)
    p_grp = 1.0 / jnp.sum(jnp.where(in_g, jnp.exp(x - m1), 0.0), axis=-1, keepdims=True)
    lo = MOE_GROUPS + grp * EXPERTS_PER_GROUP
    in_e = (lane >= lo) & (lane < lo + EXPERTS_PER_GROUP)
    v0 = jnp.max(jnp.where(in_e, x, ninf), axis=-1, keepdims=True)
    i0 = jnp.min(jnp.where(in_e & (x == v0), lane, big), axis=-1, keepdims=True)
    rest = in_e & (lane != i0)
    v1 = jnp.max(jnp.where(rest, x, ninf), axis=-1, keepdims=True)
    i1 = jnp.min(jnp.where(rest & (x == v1), lane, big), axis=-1, keepdims=True)
    e1 = jnp.exp(v1 - v0)
    g0 = p_grp / (1.0 + e1)
    g1 = p_grp * e1 / (1.0 + e1)
    idx = jnp.where(lane == 0.0, i0 - MOE_GROUPS, jnp.where(lane == 1.0, i1 - MOE_GROUPS, 0.0))
    idx_ref[...] = idx.astype(jnp.int32)
    gate_ref[...] = jnp.where(lane == 0.0, g0, jnp.where(lane == 1.0, g1, 0.0))


def _route(logits):
    n = logits.shape[0]
    tl = ROW_TILE
    spec = pl.BlockSpec((tl, LOGIT_W), lambda i: (i, 0))
    return pl.pallas_call(
        _route_kernel,
        out_shape=(jax.ShapeDtypeStruct((n, LOGIT_W), jnp.int32), jax.ShapeDtypeStruct((n, LOGIT_W), F32)),
        grid=(n // tl,),
        in_specs=[spec],
        out_specs=(spec, spec),
        compiler_params=pltpu.CompilerParams(dimension_semantics=("parallel",)),
        name="moe_route",
    )(logits)


def _dispatch_plan(expert):
    n = expert.shape[0]
    mb = MOE_ROWS
    e_flat = expert.reshape(-1)
    onehot = (e_flat[:, None] == jnp.arange(N_EXPERTS, dtype=jnp.int32)[None, :]).astype(jnp.int32)
    csum = jnp.cumsum(onehot, axis=0)
    pos = jnp.take_along_axis(csum, e_flat[:, None], axis=1)[:, 0] - 1
    counts = csum[-1]
    padded = (counts + mb - 1) // mb * mb
    pad_ends = jnp.cumsum(padded)
    pad_starts = pad_ends - padded
    dest = pad_starts[e_flat] + pos
    n_blocks = -(-(2 * n + N_EXPERTS * (mb - 1)) // mb)
    tok = jnp.repeat(jnp.arange(n, dtype=jnp.int32), 2)
    buf_tok = jnp.zeros((n_blocks * mb,), jnp.int32).at[dest].set(tok)
    blk_expert = jnp.minimum(
        jnp.searchsorted(pad_ends, jnp.arange(n_blocks, dtype=jnp.int32) * mb, side='right'),
        N_EXPERTS - 1).astype(jnp.int32)
    return dest.astype(jnp.int32), buf_tok.reshape(n_blocks, 1, mb), blk_expert


def _row_gather_start(idx_ref, src_hbm, dst, sem, nrows):
    def body(r, c):
        pltpu.make_async_copy(src_hbm.at[pl.ds(idx_ref[0, 0, r], 1)], dst.at[pl.ds(r, 1)], sem).start()
        return c
    lax.fori_loop(0, nrows, body, 0)


def _row_gather_wait(src_hbm, dst, sem, nrows):
    def body(r, c):
        pltpu.make_async_copy(src_hbm.at[pl.ds(0, 1)], dst.at[pl.ds(r, 1)], sem).wait()
        return c
    lax.fori_loop(0, nrows, body, 0)


def _ffn_kernel(be_ref, tok_ref, tokn_ref, f_hbm, w1_ref, w3_ref, w2_ref, y_ref, xbuf, sem):
    i = pl.program_id(0)
    nblk = pl.num_programs(0)
    slot = lax.rem(i, 2)

    @pl.when(i == 0)
    def _():
        _row_gather_start(tok_ref, f_hbm, xbuf.at[0], sem.at[0], MOE_ROWS)

    @pl.when(i + 1 < nblk)
    def _():
        _row_gather_start(tokn_ref, f_hbm, xbuf.at[1 - slot], sem.at[1 - slot], MOE_ROWS)

    _row_gather_wait(f_hbm, xbuf.at[slot], sem.at[slot], MOE_ROWS)
    x = xbuf[slot].astype(BF16)
    a = _dot(x, w1_ref[0])
    hid = a * jax.nn.sigmoid(a) * _dot(x, w3_ref[0])
    y_ref[...] = _dot(hid.astype(BF16), w2_ref[0])


def _expert_ffn(f, buf_tok, blk_expert, w1, w3, w2):
    n_blocks = buf_tok.shape[0]
    mb = MOE_ROWS
    idx_spec = lambda off: pl.BlockSpec((1, 1, mb), lambda i, be: (jnp.minimum(i + off, n_blocks - 1), 0, 0),
                                        memory_space=pltpu.SMEM)
    return pl.pallas_call(
        _ffn_kernel,
        out_shape=jax.ShapeDtypeStruct((n_blocks * mb, D_MODEL), F32),
        grid_spec=pltpu.PrefetchScalarGridSpec(
            num_scalar_prefetch=1,
            grid=(n_blocks,),
            in_specs=[
                idx_spec(0), idx_spec(1),
                pl.BlockSpec(memory_space=pl.ANY),
                pl.BlockSpec((1, D_MODEL, D_FF_EXPERT), lambda i, be: (be[i], 0, 0)),
                pl.BlockSpec((1, D_MODEL, D_FF_EXPERT), lambda i, be: (be[i], 0, 0)),
                pl.BlockSpec((1, D_FF_EXPERT, D_MODEL), lambda i, be: (be[i], 0, 0)),
            ],
            out_specs=pl.BlockSpec((mb, D_MODEL), lambda i, be: (i, 0)),
            scratch_shapes=[pltpu.VMEM((2, mb, D_MODEL), F32), pltpu.SemaphoreType.DMA((2,))],
        ),
        compiler_params=pltpu.CompilerParams(dimension_semantics=("arbitrary",), vmem_limit_bytes=VMEM_LIMIT),
        name="moe_expert_ffn",
    )(blk_expert, buf_tok, buf_tok, f, w1, w3, w2)


def _combine_kernel(d_ref, dn_ref, h_ref, mod_ref, gate_ref, y_hbm, o_ref, rbuf, sem):
    i = pl.program_id(0)
    nsteps = pl.num_programs(0)
    slot = lax.rem(i, 2)
    tl = ROW_TILE

    @pl.when(i == 0)
    def _():
        _row_gather_start(d_ref, y_hbm, rbuf.at[0], sem.at[0], 2 * tl)

    @pl.when(i + 1 < nsteps)
    def _():
        _row_gather_start(dn_ref, y_hbm, rbuf.at[1 - slot], sem.at[1 - slot], 2 * tl)

    _row_gather_wait(y_hbm, rbuf.at[slot], sem.at[slot], 2 * tl)
    g = gate_ref[...]
    out = g[:, 0:1] * rbuf[slot, 0:tl, :] + g[:, 1:2] * rbuf[slot, tl:2 * tl, :]
    o_ref[...] = h_ref[...] + mod_ref[0, 0][5:6] * out


def _combine(hflat, modall, gate, dest, yb, tiles_per_sample, nct):
    n = hflat.shape[0]
    tl = ROW_TILE
    nsteps = n // tl
    d2 = dest.reshape(nsteps, tl, 2).transpose(0, 2, 1).reshape(nsteps, 1, 2 * tl)
    idx_spec = lambda off: pl.BlockSpec((1, 1, 2 * tl), lambda i: (jnp.minimum(i + off, nsteps - 1), 0, 0),
                                        memory_space=pltpu.SMEM)
    row = lambda w: pl.BlockSpec((tl, w), lambda i: (i, 0))

    def mod_map(i):
        return (i // tiles_per_sample, jnp.where(i % tiles_per_sample >= nct, 1, 0), 0, 0)

    return pl.pallas_call(
        _combine_kernel,
        out_shape=jax.ShapeDtypeStruct((n, D_MODEL), F32),
        grid=(nsteps,),
        in_specs=[
            idx_spec(0), idx_spec(1), row(D_MODEL),
            pl.BlockSpec((1, 1, 8, D_MODEL), mod_map),
            row(LOGIT_W), pl.BlockSpec(memory_space=pl.ANY),
        ],
        out_specs=row(D_MODEL),
        scratch_shapes=[pltpu.VMEM((2, 2 * tl, D_MODEL), F32), pltpu.SemaphoreType.DMA((2,))],
        compiler_params=pltpu.CompilerParams(dimension_semantics=("arbitrary",), vmem_limit_bytes=VMEM_LIMIT),
        name="moe_combine",
    )(d2, d2, hflat, modall, gate, yb)


def _rope_tables(lat_len, ctx_len):
    pos = jnp.arange(lat_len, dtype=jnp.int32)
    n_freq = HEAD_DIM // 4
    inv = ROPE_BASE ** (-jnp.arange(n_freq, dtype=F32) / n_freq)
    ang_r = (pos // GRID_W).astype(F32)[:, None] * inv
    ang_c = (pos % GRID_W).astype(F32)[:, None] * inv
    zero = jnp.zeros_like(ang_r)
    cos = jnp.concatenate([jnp.cos(ang_r)] * 2 + [jnp.cos(ang_c)] * 2, axis=-1)
    sin_up = jnp.concatenate([-jnp.sin(ang_r), zero, -jnp.sin(ang_c), zero], axis=-1)
    sin_dn = jnp.concatenate([zero, jnp.sin(ang_r), zero, jnp.sin(ang_c)], axis=-1)

    def full(tab, ctx_val):
        tab = jnp.tile(tab, (1, N_HEADS))
        return jnp.concatenate([jnp.full((ctx_len, BRANCH_W), ctx_val, F32), tab], axis=0)

    return full(cos, 1.0), full(sin_up, 0.0), full(sin_dn, 0.0)


def kernel(x, c, ctx, c_ctx, mod_w, mod_b, norm1_g, norm2_g, w_in, w_branch, w_out, ret_log_decay,
           na_qk_gain, na_rpb, s5_lambda_re, s5_lambda_im, s5_log_step, s5_b_re, s5_b_im, s5_c_re,
           s5_c_im, s5_d, s5_glu_w, s5_glu_b, gqa_qk_gain, gqa_sink, router_w1, router_b1, router_w2,
           router_b2, exp_w1, exp_w3, exp_w2):
    depth = mod_w.shape[0]
    bsz, lat_len, _ = x.shape
    ctx_len = ctx.shape[1]
    t = ctx_len + lat_len
    nct = ctx_len // ROW_TILE
    assert ctx_len % ROW_TILE == 0 and lat_len % (NA_QROWS * GRID_W) == 0 and ctx_len % RET_CHUNK == 0
    assert lat_len // GRID_W >= NA_KROWS and lat_len >= GQA_KT

    hall = jnp.concatenate([ctx, x], axis=1)
    modall = _modulation(c, c_ctx, mod_w, mod_b)
    rope_tabs = _rope_tables(lat_len, ctx_len)
    ones_bd = jnp.asarray(np.kron(np.eye(N_HEADS), np.full((HEAD_DIM, HEAD_DIM), 1.0 / HEAD_DIM)), BF16)

    for i in range(depth):
        g1 = norm1_g[i].reshape(1, D_MODEL)
        g2 = norm2_g[i].reshape(1, D_MODEL)
        w_mix = w_in[i, :, :MIX_COLS].astype(BF16)
        w_gate = w_in[i, :, MIX_COLS:].astype(BF16)
        gains = jnp.concatenate([jnp.tile(na_qk_gain[i], (1, N_HEADS)), jnp.tile(gqa_qk_gain[i], (1, N_HEADS))], 0)
        ret, na, u, gqa = _inproj(hall, g1, modall[i], w_mix, rope_tabs, gains, ones_bd, nct)

        dmask, rtabs, cdec = _ret_consts(ret_log_decay[i])
        ya = _retention(ret, dmask, rtabs, cdec, ctx_len // RET_CHUNK)
        yb = _neighborhood(na, _na_bias_tables(na_rpb[i], lat_len // GRID_W), ctx_len)
        s5w, s5cp, s5scan = _s5_consts(s5_lambda_re[i], s5_lambda_im[i], s5_log_step[i], s5_b_re[i], s5_b_im[i],
                                       s5_c_re[i], s5_c_im[i])
        ys = _s5_core(u, s5w, s5cp, s5scan, ctx_len)
        yd = _window_gqa(gqa, gqa_sink[i], ctx_len)

        s5p = jnp.zeros((8, BRANCH_W), F32).at[0].set(s5_d[i]).at[1].set(s5_glu_b[i])
        rw = jnp.zeros((D_MODEL, LOGIT_W), F32).at[:, :MOE_GROUPS].set(router_w1[i])
        rw = rw.at[:, MOE_GROUPS:MOE_GROUPS + N_EXPERTS].set(router_w2[i])
        rwh, rwl = _split_bf16(rw)
        rbias = jnp.zeros((1, LOGIT_W), F32).at[0, :MOE_GROUPS].set(router_b1[i])
        rbias = rbias.at[0, MOE_GROUPS:MOE_GROUPS + N_EXPERTS].set(router_b2[i])
        hall, f, logits = _merge(hall, g1, modall[i], ya, yb, ys, u, yd, s5p, s5_glu_w[i].astype(BF16), w_gate,
                                 w_branch[i].astype(BF16), w_out[i].astype(BF16), g2, rwh, rwl, rbias, nct)

        n = bsz * t
        idx, gate = _route(logits.reshape(n, LOGIT_W))
        dest, buf_tok, blk_expert = _dispatch_plan(idx[:, :2])
        yexp = _expert_ffn(f.reshape(n, D_MODEL), buf_tok, blk_expert, exp_w1[i].astype(BF16),
                           exp_w3[i].astype(BF16), exp_w2[i].astype(BF16))
        hall = _combine(hall.reshape(n, D_MODEL), modall[i], gate, dest, yexp, t // ROW_TILE, nct)
        hall = hall.reshape(bsz, t, D_MODEL)

    return hall[:, ctx_len:]
```

```python
import functools
import math

import numpy as np
import jax
import jax.numpy as jnp
from jax import lax
from jax.experimental import pallas as pl
from jax.experimental.pallas import tpu as pltpu

F32 = jnp.float32
BF16 = jnp.bfloat16

D_MODEL = 1024
GRID_W = 64
HEAD_DIM = 64
N_BRANCH = 4
BRANCH_W = D_MODEL // 4
N_HEADS = BRANCH_W // HEAD_DIM
RET_CHUNK = 128
NA_KH = 8
NA_KW = 16
NA_QROWS = 4
NA_KROWS = NA_QROWS + NA_KH
S5_GROUP_CH = 16
S5_GROUPS = BRANCH_W // S5_GROUP_CH
S5_STATE = 64
S5_CHUNK = 16
GQA_KV_HEADS = 2
GQA_KV_W = GQA_KV_HEADS * HEAD_DIM
WIN = 128
MOE_GROUPS = 4
EXPERTS_PER_GROUP = 8
N_EXPERTS = MOE_GROUPS * EXPERTS_PER_GROUP
D_FF_EXPERT = D_MODEL // 2
ROPE_BASE = 10000.0
EPS = 1e-6
NEG = -1e30
MIX_COLS = 9 * BRANCH_W + 2 * GQA_KV_W

ROW_TILE = 256
MOE_ROWS = 256
LOGIT_W = 128
VMEM_LIMIT = 56 * 1024 * 1024


def _const_spec(shape):
    nd = len(shape)
    return pl.BlockSpec(shape, lambda *_: (0,) * nd)


def _dot(a, b):
    return jnp.dot(a, b, preferred_element_type=F32)


def _dot_nt(a, b):
    return lax.dot_general(a, b, (((1,), (1,)), ((), ())), preferred_element_type=F32)


def _dot_tn(a, b):
    return lax.dot_general(a, b, (((0,), (0,)), ((), ())), preferred_element_type=F32)


def _split_bf16(x):
    hi = x.astype(BF16)
    lo = (x - hi.astype(F32)).astype(BF16)
    return hi, lo


def _norm_modulate(x, gain, mod, shift_row, scale_row):
    ms = jnp.mean(x * x, axis=-1, keepdims=True)
    a = x * lax.rsqrt(ms + EPS) * gain
    return a * (1.0 + mod[scale_row:scale_row + 1]) + mod[shift_row:shift_row + 1]


def _mod_kernel(c_ref, w_ref, b_ref, o_ref):
    c = c_ref[...]
    s = c * jax.nn.sigmoid(c)
    o_ref[0] = _dot(s.astype(BF16), w_ref[0].astype(BF16)) + b_ref[0]


def _modulation(c, c_ctx, mod_w, mod_b):
    depth = mod_w.shape[0]
    b = c.shape[0]
    rows = -(-(b + 1) // 8) * 8
    cs = jnp.zeros((rows, D_MODEL), F32).at[:b].set(c).at[b].set(c_ctx)
    out = pl.pallas_call(
        _mod_kernel,
        out_shape=jax.ShapeDtypeStruct((depth, rows, 6 * D_MODEL), F32),
        grid=(depth, 6),
        in_specs=[
            _const_spec((rows, D_MODEL)),
            pl.BlockSpec((1, D_MODEL, D_MODEL), lambda l, j: (l, 0, j)),
            pl.BlockSpec((1, 1, D_MODEL), lambda l, j: (l, 0, j)),
        ],
        out_specs=pl.BlockSpec((1, rows, D_MODEL), lambda l, j: (l, 0, j)),
        compiler_params=pltpu.CompilerParams(dimension_semantics=("parallel", "parallel")),
        name="adaln_mod",
    )(cs, mod_w, mod_b.reshape(depth, 1, 6 * D_MODEL))
    lat = out[:, :b].reshape(depth, b, 6, D_MODEL)
    ctx = jnp.broadcast_to(out[:, b].reshape(depth, 1, 6, D_MODEL), (depth, b, 6, D_MODEL))
    modall = jnp.stack([ctx, lat], axis=2)
    return jnp.pad(modall, ((0, 0), (0, 0), (0, 0), (0, 2), (0, 0)))


def _head_mean_sq(x, ones_bd):
    hi, lo = _split_bf16(x * x)
    return _dot(hi, ones_bd) + _dot(lo, ones_bd)


def _rope(x, cos, sin_up, sin_dn):
    w = x.shape[-1]
    return x * cos + pltpu.roll(x, w - 16, 1) * sin_up + pltpu.roll(x, 16, 1) * sin_dn


def _inproj_kernel(h_ref, g1_ref, mod_ref, w_ref, cos_ref, sup_ref, sdn_ref, gains_ref, ones_ref,
                   ret_ref, na_ref, u_ref, gqa_ref):
    a = _norm_modulate(h_ref[0], g1_ref[...], mod_ref[0, 0], 0, 1)
    p = _dot(a.astype(BF16), w_ref[...])
    cos, sup, sdn = cos_ref[...], sup_ref[...], sdn_ref[...]
    ones_bd = ones_ref[...]
    scale = HEAD_DIM ** -0.5
    bw = BRANCH_W

    def col(i):
        return p[:, i * bw:(i + 1) * bw]

    def head_rms(x, gain):
        w = x.shape[-1]
        ms = _head_mean_sq(x, ones_bd[:w, :w])
        return x * lax.rsqrt(ms + EPS) * gain

    ret_ref[0, :, 0 * bw:1 * bw] = (_rope(col(0), cos, sup, sdn) * scale).astype(BF16)
    ret_ref[0, :, 1 * bw:2 * bw] = _rope(col(1), cos, sup, sdn).astype(BF16)
    ret_ref[0, :, 2 * bw:3 * bw] = col(2).astype(BF16)
    ret_ref[0, :, 3 * bw:4 * bw] = col(3).astype(BF16)
    na_ref[0, :, 0 * bw:1 * bw] = (head_rms(col(4), gains_ref[0:1]) * scale).astype(BF16)
    na_ref[0, :, 1 * bw:2 * bw] = head_rms(col(5), gains_ref[1:2]).astype(BF16)
    na_ref[0, :, 2 * bw:3 * bw] = col(6).astype(BF16)
    u_ref[0] = col(7)
    gq = _rope(head_rms(col(8), gains_ref[2:3]), cos, sup, sdn) * scale
    gqa_ref[0, :, 0:bw] = gq.astype(BF16)
    kw = GQA_KV_W
    gk = head_rms(p[:, 9 * bw:9 * bw + kw], gains_ref[3:4, :kw])
    gk = _rope(gk, cos[:, :kw], sup[:, :kw], sdn[:, :kw])
    gqa_ref[0, :, bw:bw + kw] = gk.astype(BF16)
    gqa_ref[0, :, bw + kw:bw + 2 * kw] = p[:, 9 * bw + kw:9 * bw + 2 * kw].astype(BF16)


def _mod_spec(nct):
    return pl.BlockSpec((1, 1, 8, D_MODEL), lambda b, t: (b, jnp.where(t >= nct, 1, 0), 0, 0))


def _inproj(hall, g1, modall, w_mix, rope_tabs, gains, ones_bd, nct):
    b, t, _ = hall.shape
    tl = ROW_TILE
    row = lambda w: pl.BlockSpec((1, tl, w), lambda b, t: (b, t, 0))
    tab = pl.BlockSpec((tl, BRANCH_W), lambda b, t: (t, 0))
    return pl.pallas_call(
        _inproj_kernel,
        out_shape=(
            jax.ShapeDtypeStruct((b, t, 4 * BRANCH_W), BF16),
            jax.ShapeDtypeStruct((b, t, 3 * BRANCH_W), BF16),
            jax.ShapeDtypeStruct((b, t, BRANCH_W), F32),
            jax.ShapeDtypeStruct((b, t, BRANCH_W + 2 * GQA_KV_W), BF16),
        ),
        grid=(b, t // tl),
        in_specs=[
            row(D_MODEL), _const_spec((1, D_MODEL)), _mod_spec(nct), _const_spec((D_MODEL, MIX_COLS)),
            tab, tab, tab, _const_spec((4, BRANCH_W)), _const_spec((BRANCH_W, BRANCH_W)),
        ],
        out_specs=(row(4 * BRANCH_W), row(3 * BRANCH_W), row(BRANCH_W), row(BRANCH_W + 2 * GQA_KV_W)),
        compiler_params=pltpu.CompilerParams(dimension_semantics=("parallel", "parallel"),
                                             vmem_limit_bytes=VMEM_LIMIT),
        name="inproj",
    )(hall, g1, modall, w_mix, *rope_tabs, gains, ones_bd)


def _ret_consts(log_decay):
    lg = -jnp.exp(log_decay.astype(F32))
    idx = jnp.arange(RET_CHUNK, dtype=F32)
    diff = idx[:, None] - idx[None, :]
    fwd = jnp.where(diff > 0, jnp.exp(jnp.maximum(diff, 0.0)[None] * lg[0][:, None, None]), 0.0)
    bwd = jnp.where(diff < 0, jnp.exp(jnp.maximum(-diff, 0.0)[None] * lg[1][:, None, None]), 0.0)
    dmask = fwd + bwd + 2.0 * jnp.eye(RET_CHUNK, dtype=F32)[None]
    tabs = jnp.stack([
        jnp.exp((RET_CHUNK - 1 - idx)[None, :] * lg[0][:, None]),
        jnp.exp(idx[None, :] * lg[1][:, None]),
        jnp.exp((idx + 1.0)[None, :] * lg[0][:, None]),
        jnp.exp((RET_CHUNK - idx)[None, :] * lg[1][:, None]),
    ])
    tabs = jnp.repeat(tabs.transpose(0, 2, 1), HEAD_DIM, axis=-1)
    cdec = jnp.exp(RET_CHUNK * lg).reshape(2 * N_HEADS)
    return dmask, tabs, cdec


def _ret_bwd_chunk(i, ncc, nch):
    return jnp.where(i < ncc, ncc - 1 - i, nch - 1 - (i - ncc))


def _ret_kernel(cdec_ref, q_ref, k_ref, v_ref, g_ref, dmask_ref, tab_ref, o_ref, sf, sb, sbprev, *, ncc):
    ph = pl.program_id(1)
    i = pl.program_id(2)
    nch = pl.num_programs(2)
    hd = HEAD_DIM

    @pl.when(jnp.logical_and(ph == 0, i == 0))
    def _():
        sb[...] = jnp.zeros_like(sb)

    @pl.when(jnp.logical_and(ph == 1, i == 0))
    def _():
        sf[...] = jnp.zeros_like(sf)

    k = k_ref[0]
    v = v_ref[0]

    @pl.when(ph == 0)
    def _():
        n = _ret_bwd_chunk(i, ncc, nch)
        kd = (k.astype(F32) * tab_ref[1]).astype(BF16)
        kv = _dot_tn(kd, v)
        for h in range(N_HEADS):
            sbprev[n, h] = sb[h]
            sb[h] = cdec_ref[N_HEADS + h] * sb[h] + kv[h * hd:(h + 1) * hd, h * hd:(h + 1) * hd]

    @pl.when(ph == 1)
    def _():
        q = q_ref[0]
        qf32 = q.astype(F32)
        qf = (qf32 * tab_ref[2]).astype(BF16)
        qb = (qf32 * tab_ref[3]).astype(BF16)
        kd = (k.astype(F32) * tab_ref[0]).astype(BF16)
        kv = _dot_tn(kd, v)
        outs = []
        for h in range(N_HEADS):
            hs = slice(h * hd, (h + 1) * hd)
            s = _dot_nt(q[:, hs], k[:, hs]) * dmask_ref[h]
            o = _dot(s.astype(BF16), v[:, hs])
            o = o + _dot(qf[:, hs], sf[h].astype(BF16)) + _dot(qb[:, hs], sbprev[i, h].astype(BF16))
            sf[h] = cdec_ref[h] * sf[h] + kv[hs, hs]
            outs.append(o * lax.rsqrt(jnp.mean(o * o, axis=-1, keepdims=True) + EPS))
        g = g_ref[0].astype(F32)
        o_ref[0] = (jnp.concatenate(outs, axis=-1) * (g * jax.nn.sigmoid(g))).astype(BF16)


def _retention(ret, dmask, tabs, cdec, ncc):
    b, t, _ = ret.shape
    nch = t // RET_CHUNK
    bw = BRANCH_W

    def kv_map(col):
        return lambda b, ph, i, *_: (b, jnp.where(ph == 0, _ret_bwd_chunk(i, ncc, nch), i), col)

    def q_map(col):
        return lambda b, ph, i, *_: (b, jnp.where(ph == 0, 0, i), col)

    blk = (1, RET_CHUNK, bw)
    return pl.pallas_call(
        functools.partial(_ret_kernel, ncc=ncc),
        out_shape=jax.ShapeDtypeStruct((b, t, bw), BF16),
        grid_spec=pltpu.PrefetchScalarGridSpec(
            num_scalar_prefetch=1,
            grid=(b, 2, nch),
            in_specs=[
                pl.BlockSpec(blk, q_map(0)), pl.BlockSpec(blk, kv_map(1)), pl.BlockSpec(blk, kv_map(2)),
                pl.BlockSpec(blk, q_map(3)),
                _const_spec((N_HEADS, RET_CHUNK, RET_CHUNK)), _const_spec((4, RET_CHUNK, bw)),
            ],
            out_specs=pl.BlockSpec(blk, q_map(0)),
            scratch_shapes=[
                pltpu.VMEM((N_HEADS, HEAD_DIM, HEAD_DIM), F32),
                pltpu.VMEM((N_HEADS, HEAD_DIM, HEAD_DIM), F32),
                pltpu.VMEM((nch, N_HEADS, HEAD_DIM, HEAD_DIM), F32),
            ],
        ),
        compiler_params=pltpu.CompilerParams(dimension_semantics=("parallel", "arbitrary", "arbitrary"),
                                             vmem_limit_bytes=VMEM_LIMIT),
        name="retention",
    )(cdec, ret, ret, ret, ret, dmask, tabs)


def _na_bias_tables(rpb, rows):
    nrb = rows // NA_QROWS
    qi = np.arange(NA_QROWS)
    kl = np.arange(NA_KROWS)
    ridx, rvalid = [], []
    for rb in (0, 1, nrb - 1):
        u0 = int(np.clip(NA_QROWS * rb - NA_KH // 2, 0, rows - NA_KROWS))
        r = NA_QROWS * rb + qi[:, None]
        kr = u0 + kl[None, :]
        r0 = np.clip(r - NA_KH // 2, 0, rows - NA_KH)
        rvalid.append((kr >= r0) & (kr < r0 + NA_KH))
        ridx.append(np.clip(kr - r + NA_KH - 1, 0, 2 * NA_KH - 2))
    ridx, rvalid = np.stack(ridx), np.stack(rvalid)
    qc = np.arange(GRID_W)[:, None]
    kc = np.arange(GRID_W)[None, :]
    ws = np.clip(qc - NA_KW // 2, 0, GRID_W - NA_KW)
    cvalid = (kc >= ws) & (kc < ws + NA_KW)
    cidx = np.clip(kc - qc + NA_KW - 1, 0, 2 * NA_KW - 2)
    row_sel = jnp.asarray(ridx[..., None] == np.arange(2 * NA_KH - 1), F32)
    col_sel = jnp.asarray(cidx[..., None] == np.arange(2 * NA_KW - 1), F32)
    hp = lax.Precision.HIGHEST
    rows_b = jnp.einsum('tikr,hrc->thikc', row_sel, rpb.astype(F32), precision=hp)
    bias = jnp.einsum('thikc,qzc->thiqkz', rows_b, col_sel, precision=hp)
    valid = rvalid[:, None, :, None, :, None] & cvalid[None, None, None, :, None, :]
    bias = jnp.where(jnp.asarray(valid), bias, NEG)
    return bias.reshape(3, N_HEADS, NA_QROWS * GRID_W, NA_KROWS * GRID_W)


def _softmax_pv(parts, sink=None):
    m = functools.reduce(jnp.maximum, [jnp.max(s, axis=-1, keepdims=True) for s, _ in parts])
    if sink is not None:
        m = jnp.maximum(m, sink)
    den = None
    acc = None
    for s, v in parts:
        p = jnp.exp(s - m)
        d = jnp.sum(p, axis=-1, keepdims=True)
        a = _dot(p.astype(BF16), v)
        den = d if den is None else den + d
        acc = a if acc is None else acc + a
    if sink is not None:
        den = den + jnp.exp(sink - m)
    return acc / den


def _na_kernel(q_ref, k_ref, v_ref, bias_ref, o_ref, *, nct, ctx_len, rows):
    j = pl.program_id(1)
    hd = HEAD_DIM
    q = q_ref[0]
    kc = k_ref[0, 0:ctx_len, :]
    vc = v_ref[0, 0:ctx_len, :]

    @pl.when(j < nct)
    def _():
        outs = []
        for h in range(N_HEADS):
            hs = slice(h * hd, (h + 1) * hd)
            outs.append(_softmax_pv([(_dot_nt(q[:, hs], kc[:, hs]), vc[:, hs])]))
        o_ref[0] = jnp.concatenate(outs, axis=-1).astype(BF16)

    @pl.when(j >= nct)
    def _():
        rb = j - nct
        u0 = jnp.clip(NA_QROWS * rb - NA_KH // 2, 0, rows - NA_KROWS)
        start = pl.multiple_of(ctx_len + u0 * GRID_W, GRID_W)
        kw = k_ref[0, pl.ds(start, NA_KROWS * GRID_W), :]
        vw = v_ref[0, pl.ds(start, NA_KROWS * GRID_W), :]
        outs = []
        for h in range(N_HEADS):
            hs = slice(h * hd, (h + 1) * hd)
            s_nb = _dot_nt(q[:, hs], kw[:, hs]) + bias_ref[0, h]
            s_cx = _dot_nt(q[:, hs], kc[:, hs])
            outs.append(_softmax_pv([(s_nb, vw[:, hs]), (s_cx, vc[:, hs])]))
        o_ref[0] = jnp.concatenate(outs, axis=-1).astype(BF16)


def _neighborhood(na, bias, ctx_len):
    b, t, _ = na.shape
    bw = BRANCH_W
    qt = NA_QROWS * GRID_W
    nct = ctx_len // qt
    nsteps = t // qt
    nrb = nsteps - nct
    rows = nrb * NA_QROWS

    def bias_map(b, j):
        rb = j - nct
        return (jnp.where(rb <= 0, 0, jnp.where(rb == nrb - 1, 2, 1)), 0, 0, 0)

    return pl.pallas_call(
        functools.partial(_na_kernel, nct=nct, ctx_len=ctx_len, rows=rows),
        out_shape=jax.ShapeDtypeStruct((b, t, bw), BF16),
        grid=(b, nsteps),
        in_specs=[
            pl.BlockSpec((1, qt, bw), lambda b, j: (b, j, 0)),
            pl.BlockSpec((1, t, bw), lambda b, j: (b, 0, 1)),
            pl.BlockSpec((1, t, bw), lambda b, j: (b, 0, 2)),
            pl.BlockSpec((1, N_HEADS, qt, NA_KROWS * GRID_W), bias_map),
        ],
        out_specs=pl.BlockSpec((1, qt, bw), lambda b, j: (b, j, 0)),
        compiler_params=pltpu.CompilerParams(dimension_semantics=("parallel", "arbitrary"),
                                             vmem_limit_bytes=VMEM_LIMIT),
        name="neighborhood_attn",
    )(na, na, na, bias)


GQA_QT = 256
GQA_KT = GQA_QT + 2 * WIN


def _gqa_kernel(sink_ref, q_ref, kv_ref, o_ref, *, nct, ctx_len, lat_len):
    j = pl.program_id(1)
    hd = HEAD_DIM
    grp = N_HEADS // GQA_KV_HEADS
    q = q_ref[0]
    kvc = kv_ref[0, 0:ctx_len, :]

    def stacked_q(kvh):
        return jnp.concatenate([q[:, (kvh * grp + g) * hd:(kvh * grp + g + 1) * hd] for g in range(grp)], axis=0)

    def sink_col(kvh):
        return jnp.concatenate([jnp.full((GQA_QT, 1), sink_ref[kvh * grp + g], F32) for g in range(grp)], axis=0)

    def emit(results):
        cols = [results[kvh][g * GQA_QT:(g + 1) * GQA_QT] for kvh in range(GQA_KV_HEADS) for g in range(grp)]
        o_ref[0] = jnp.concatenate(cols, axis=-1).astype(BF16)

    @pl.when(j < nct)
    def _():
        res = []
        for kvh in range(GQA_KV_HEADS):
            kc = kvc[:, kvh * hd:(kvh + 1) * hd]
            vc = kvc[:, GQA_KV_W + kvh * hd:GQA_KV_W + (kvh + 1) * hd]
            res.append(_softmax_pv([(_dot_nt(stacked_q(kvh), kc), vc)], sink=sink_col(kvh)))
        emit(res)

    @pl.when(j >= nct)
    def _():
        q0 = (j - nct) * GQA_QT
        k0 = jnp.clip(q0 - WIN, 0, lat_len - GQA_KT)
        kvw = kv_ref[0, pl.ds(pl.multiple_of(ctx_len + k0, WIN), GQA_KT), :]
        qpos = q0 + (lax.broadcasted_iota(jnp.int32, (grp * GQA_QT, GQA_KT), 0) & (GQA_QT - 1))
        kpos = k0 + lax.broadcasted_iota(jnp.int32, (grp * GQA_QT, GQA_KT), 1)
        band = jnp.abs(qpos - kpos) <= WIN
        res = []
        for kvh in range(GQA_KV_HEADS):
            ks = slice(kvh * hd, (kvh + 1) * hd)
            vs = slice(GQA_KV_W + kvh * hd, GQA_KV_W + (kvh + 1) * hd)
            q2 = stacked_q(kvh)
            s_w = jnp.where(band, _dot_nt(q2, kvw[:, ks]), NEG)
            s_c = _dot_nt(q2, kvc[:, ks])
            res.append(_softmax_pv([(s_w, kvw[:, vs]), (s_c, kvc[:, vs])], sink=sink_col(kvh)))
        emit(res)


def _window_gqa(gqa, sink, ctx_len):
    b, t, _ = gqa.shape
    bw = BRANCH_W
    nct = ctx_len // GQA_QT
    return pl.pallas_call(
        functools.partial(_gqa_kernel, nct=nct, ctx_len=ctx_len, lat_len=t - ctx_len),
        out_shape=jax.ShapeDtypeStruct((b, t, bw), BF16),
        grid_spec=pltpu.PrefetchScalarGridSpec(
            num_scalar_prefetch=1,
            grid=(b, t // GQA_QT),
            in_specs=[
                pl.BlockSpec((1, GQA_QT, bw), lambda b, j, *_: (b, j, 0)),
                pl.BlockSpec((1, t, bw), lambda b, j, *_: (b, 0, 1)),
            ],
            out_specs=pl.BlockSpec((1, GQA_QT, bw), lambda b, j, *_: (b, j, 0)),
        ),
        compiler_params=pltpu.CompilerParams(dimension_semantics=("parallel", "arbitrary"),
                                             vmem_limit_bytes=VMEM_LIMIT),
        name="window_gqa",
    )(sink.astype(F32), gqa, gqa)


def _cmul(ar, ai, br, bi):
    return ar * br - ai * bi, ar * bi + ai * br


def _s5_consts(lam_re, lam_im, log_step, b_re, b_im, c_re, c_im):
    f32 = F32
    tc = S5_CHUNK
    cre, cim = c_re.astype(f32), c_im.astype(f32)
    bre, bim = b_re.astype(f32), b_im.astype(f32)
    kern, inj, cpow, scan = [], [], [], []
    for d in range(2):
        lre, lim, dt = lam_re[d].astype(f32), lam_im[d].astype(f32), jnp.exp(log_step[d].astype(f32))[:, None]
        mag = jnp.exp(lre * dt)
        ar, ai = mag * jnp.cos(lim * dt), mag * jnp.sin(lim * dt)
        den = lre * lre + lim * lim
        nr = ar - 1.0
        coef_re = (nr * lre + ai * lim) / den
        coef_im = (ai * lre - nr * lim) / den
        bbr = coef_re[..., None] * bre - coef_im[..., None] * bim
        bbi = coef_re[..., None] * bim + coef_im[..., None] * bre
        pr, pi = [jnp.ones_like(ar)], [jnp.zeros_like(ar)]
        for _ in range(tc):
            nr_, ni_ = _cmul(pr[-1], pi[-1], ar, ai)
            pr.append(nr_)
            pi.append(ni_)
        pr, pi = jnp.stack(pr), jnp.stack(pi)
        lbr, lbi = _cmul(pr[:tc, :, :, None], pi[:tc, :, :, None], bbr[None], bbi[None])
        kern.append(jnp.einsum('gcp,tgpd->tgcd', cre, lbr, precision=lax.Precision.HIGHEST)
                    - jnp.einsum('gcp,tgpd->tgcd', cim, lbi, precision=lax.Precision.HIGHEST))
        inj.append((lbr[::-1], lbi[::-1]) if d == 0 else (lbr, lbi))
        pwr, pwi = (pr[1:], pi[1:]) if d == 0 else (pr[1:][::-1], pi[1:][::-1])
        dr, di = _cmul(cre[None], cim[None], pwr[:, :, None, :], pwi[:, :, None, :])
        cpow.append((dr, -di))
        scan.append((pr[tc], pi[tc]))
    g, gc, p = S5_GROUPS, S5_GROUP_CH, S5_STATE
    lag = np.arange(tc)[None, :] - np.arange(tc)[:, None]
    taus = np.arange(tc)[:, None, None]
    sel_f = jnp.asarray(lag[None] == taus, F32)
    sel_b = jnp.asarray(-lag[None] == taus, F32)
    hp = lax.Precision.HIGHEST
    toep = (jnp.einsum('xst,xgcd->gsdtc', sel_f, kern[0], precision=hp)
            + jnp.einsum('xst,xgcd->gsdtc', sel_b, kern[1], precision=hp))
    toep = toep.reshape(g, tc * gc, tc * gc)
    injm = jnp.concatenate([x.transpose(1, 0, 3, 2).reshape(g, tc * gc, p)
                            for x in (inj[0][0], inj[0][1], inj[1][0], inj[1][1])], axis=-1)
    w = jnp.concatenate([toep, injm], axis=-1)
    cp = jnp.concatenate([x.transpose(1, 3, 0, 2).reshape(g, p, tc * gc)
                          for x in (cpow[0][0], cpow[0][1], cpow[1][0], cpow[1][1])], axis=1)
    rows = []
    for d in range(2):
        sr, si = scan[d]
        rows += [jnp.concatenate([sr, sr], -1), jnp.concatenate([-si, si], -1)]
    scan_c = jnp.stack(rows, axis=1)
    scan_c = jnp.pad(scan_c, ((0, 0), (0, 4), (0, 0)))
    return w.astype(BF16), cp.astype(BF16), scan_c


def _s5_kernel(u_ref, w_ref, cp_ref, sc_ref, y_ref, inj, injs, xs, *, bsz, nch, ncc):
    p2 = 2 * S5_STATE
    r = _dot(u_ref[0], w_ref[0])
    y_ref[0] = r[:, :BRANCH_W]
    rf = r[:, BRANCH_W:BRANCH_W + p2]
    rb = r[:, BRANCH_W + p2:]
    inj[:, 0:p2] = rf
    inj[:, p2:] = rb
    injs[:, 0:p2] = pltpu.roll(rf, S5_STATE, 1)
    injs[:, p2:] = pltpu.roll(rb, S5_STATE, 1)
    a_f, s_f, a_b, s_b = sc_ref[0, 0:1], sc_ref[0, 1:2], sc_ref[0, 2:3], sc_ref[0, 3:4]

    def body(n, carry):
        xf, xfs, xb, xbs = carry
        rowf = pl.multiple_of(n * bsz, bsz)
        nb = jnp.where(n < ncc, ncc - 1 - n, nch - 1 - (n - ncc))
        rowb = pl.multiple_of(nb * bsz, bsz)
        xs[pl.ds(rowf, bsz), 0:p2] = xf
        xs[pl.ds(rowb, bsz), p2:] = xb
        nxf = a_f * xf + s_f * xfs + inj[pl.ds(rowf, bsz), 0:p2]
        nxfs = a_f * xfs - s_f * xf + injs[pl.ds(rowf, bsz), 0:p2]
        nxb = a_b * xb + s_b * xbs + inj[pl.ds(rowb, bsz), p2:]
        nxbs = a_b * xbs - s_b * xb + injs[pl.ds(rowb, bsz), p2:]
        return nxf, nxfs, nxb, nxbs

    z = jnp.zeros((bsz, p2), F32)
    lax.fori_loop(0, nch, body, (z, z, z, z))
    y_ref[0] += _dot(xs[...].astype(BF16), cp_ref[0])


def _s5_core(u, w, cp, scan_c, ctx_len):
    b, t, bw = u.shape
    g, gc, tc = S5_GROUPS, S5_GROUP_CH, S5_CHUNK
    nch = t // tc
    rows = nch * b
    ug = u.reshape(b, nch, tc, g, gc).transpose(3, 1, 0, 2, 4).reshape(g, rows, tc * gc).astype(BF16)
    y = pl.pallas_call(
        functools.partial(_s5_kernel, bsz=b, nch=nch, ncc=ctx_len // tc),
        out_shape=jax.ShapeDtypeStruct((g, rows, bw), F32),
        grid=(g,),
        in_specs=[
            pl.BlockSpec((1, rows, bw), lambda i: (i, 0, 0)),
            pl.BlockSpec((1, bw, bw + 4 * S5_STATE), lambda i: (i, 0, 0)),
            pl.BlockSpec((1, 4 * S5_STATE, bw), lambda i: (i, 0, 0)),
            pl.BlockSpec((1, 8, 2 * S5_STATE), lambda i: (i, 0, 0)),
        ],
        out_specs=pl.BlockSpec((1, rows, bw), lambda i: (i, 0, 0)),
        scratch_shapes=[pltpu.VMEM((rows, 4 * S5_STATE), F32)] * 3,
        compiler_params=pltpu.CompilerParams(dimension_semantics=("parallel",), vmem_limit_bytes=VMEM_LIMIT),
        name="s5_chunked",
    )(ug, w, cp, scan_c)
    return y.reshape(g, nch, b, tc, gc).transpose(2, 1, 3, 0, 4).reshape(b, t, bw)


def _merge_kernel(h_ref, g1_ref, mod_ref, ya_ref, yb_ref, ys_ref, u_ref, yd_ref, s5p_ref, gluw_ref,
                  wg_ref, wb_ref, wo_ref, g2_ref, rwh_ref, rwl_ref, rb_ref,
                  hout_ref, f_ref, lg_ref):
    x = h_ref[0]
    mod = mod_ref[0, 0]
    a = _norm_modulate(x, g1_ref[...], mod, 0, 1)
    gates = _dot(a.astype(BF16), wg_ref[...])
    z = ys_ref[0] + s5p_ref[0:1] * u_ref[0]
    z = jax.nn.gelu(z)
    yc = z * jax.nn.sigmoid(_dot(z.astype(BF16), gluw_ref[...]) + s5p_ref[1:2])
    ys = (ya_ref[0], yb_ref[0], yc.astype(BF16), yd_ref[0])
    merged = None
    for i in range(N_BRANCH):
        term = jax.nn.sigmoid(gates[:, i * D_MODEL:(i + 1) * D_MODEL]) * _dot(ys[i], wb_ref[i])
        merged = term if merged is None else merged + term
    hn = x + mod[2:3] * _dot(merged.astype(BF16), wo_ref[...])
    hout_ref[0] = hn
    f = _norm_modulate(hn, g2_ref[...], mod, 3, 4)
    f_ref[0] = f
    fh, fl = _split_bf16(f)
    lg_ref[0] = _dot(fh, rwh_ref[...]) + _dot(fl, rwh_ref[...]) + _dot(fh, rwl_ref[...]) + rb_ref[...]


def _merge(hall, g1, modall, ya, yb, ys, u, yd, s5p, gluw, wg, wb, wo, g2, rwh, rwl, rbias, nct):
    b, t, _ = hall.shape
    tl = ROW_TILE
    row = lambda w: pl.BlockSpec((1, tl, w), lambda b, t: (b, t, 0))
    return pl.pallas_call(
        _merge_kernel,
        out_shape=(
            jax.ShapeDtypeStruct((b, t, D_MODEL), F32),
            jax.ShapeDtypeStruct((b, t, D_MODEL), F32),
            jax.ShapeDtypeStruct((b, t, LOGIT_W), F32),
        ),
        grid=(b, t // tl),
        in_specs=[
            row(D_MODEL), _const_spec((1, D_MODEL)), _mod_spec(nct),
            row(BRANCH_W), row(BRANCH_W), row(BRANCH_W), row(BRANCH_W), row(BRANCH_W),
            _const_spec((8, BRANCH_W)), _const_spec((BRANCH_W, BRANCH_W)),
            _const_spec((D_MODEL, N_BRANCH * D_MODEL)), _const_spec((N_BRANCH, BRANCH_W, D_MODEL)),
            _const_spec((D_MODEL, D_MODEL)), _const_spec((1, D_MODEL)),
            _const_spec((D_MODEL, LOGIT_W)), _const_spec((D_MODEL, LOGIT_W)), _const_spec((1, LOGIT_W)),
        ],
        out_specs=(row(D_MODEL), row(D_MODEL), row(LOGIT_W)),
        compiler_params=pltpu.CompilerParams(dimension_semantics=("parallel", "parallel"),
                                             vmem_limit_bytes=VMEM_LIMIT),
        name="merge_router",
    )(hall, g1, modall, ya, yb, ys, u, yd, s5p, gluw, wg, wb, wo, g2, rwh, rwl, rbias)


def _route_kernel(lg_ref, idx_ref, gate_ref):
    x = lg_ref[...]
    lane = lax.broadcasted_iota(jnp.int32, x.shape, 1).astype(F32)
    big = 1e9
    ninf = -jnp.inf
    in_g = lane < MOE_GROUPS
    m1 = jnp.max(jnp.where(in_g, x, ninf), axis=-1, keepdims=True)
    grp = jnp.min(jnp.where(in_g & (x == m1), lane, big), axis=-1, keepdims=True)
    p_grp = 1.0 / jnp.sum(jnp.where(in_g, jnp.exp(x - m1), 0.0), axis=-1, keepdims=True)
    lo = MOE_GROUPS + grp * EXPERTS_PER_GROUP
    in_e = (lane >= lo) & (lane < lo + EXPERTS_PER_GROUP)
    v0 = jnp.max(jnp.where(in_e, x, ninf), axis=-1, keepdims=True)
    i0 = jnp.min(jnp.where(in_e & (x == v0), lane, big), axis=-1, keepdims=True)
    rest = in_e & (lane != i0)
    v1 = jnp.max(jnp.where(rest, x, ninf), axis=-1, keepdims=True)
    i1 = jnp.min(jnp.where(rest & (x == v1), lane, big), axis=-1, keepdims=True)
    e1 = jnp.exp(v1 - v0)
    g0 = p_grp / (1.0 + e1)
    g1 = p_grp * e1 / (1.0 + e1)
    idx = jnp.where(lane == 0.0, i0 - MOE_GROUPS, jnp.where(lane == 1.0, i1 - MOE_GROUPS, 0.0))
    idx_ref[...] = idx.astype(jnp.int32)
    gate_ref[...] = jnp.where(lane == 0.0, g0, jnp.where(lane == 1.0, g1, 0.0))


def _route(logits):
    n = logits.shape[0]
    tl = ROW_TILE
    spec = pl.BlockSpec((tl, LOGIT_W), lambda i: (i, 0))
    return pl.pallas_call(
        _route_kernel,
        out_shape=(jax.ShapeDtypeStruct((n, LOGIT_W), jnp.int32), jax.ShapeDtypeStruct((n, LOGIT_W), F32)),
        grid=(n // tl,),
        in_specs=[spec],
        out_specs=(spec, spec),
        compiler_params=pltpu.CompilerParams(dimension_semantics=("parallel",)),
        name="moe_route",
    )(logits)


def _dispatch_plan(expert):
    n = expert.shape[0]
    mb = MOE_ROWS
    e_flat = expert.reshape(-1)
    onehot = (e_flat[:, None] == jnp.arange(N_EXPERTS, dtype=jnp.int32)[None, :]).astype(jnp.int32)
    csum = jnp.cumsum(onehot, axis=0)
    counts = csum[-1]
    padded = (counts + mb - 1) // mb * mb
    pad_ends = jnp.cumsum(padded)
    pad_starts = pad_ends - padded
    dest = jnp.sum(onehot * (csum - 1 + pad_starts[None, :]), axis=1)
    n_blocks = -(-(2 * n + N_EXPERTS * (mb - 1)) // mb)
    tok = jnp.repeat(jnp.arange(n, dtype=jnp.int32), 2)
    buf_tok = jnp.zeros((n_blocks * mb,), jnp.int32).at[dest].set(tok)
    blk_expert = jnp.minimum(
        jnp.searchsorted(pad_ends, jnp.arange(n_blocks, dtype=jnp.int32) * mb, side='right'),
        N_EXPERTS - 1).astype(jnp.int32)
    return dest.astype(jnp.int32), buf_tok.reshape(n_blocks, 1, mb), blk_expert


GATHER_UNROLL = 32


def _row_gather_start(idx_ref, src_hbm, dst, sem, nrows):
    def body(c, carry):
        base = c * GATHER_UNROLL
        for j in range(GATHER_UNROLL):
            r = base + j
            pltpu.make_async_copy(src_hbm.at[pl.ds(idx_ref[0, 0, r], 1)], dst.at[pl.ds(r, 1)], sem).start()
        return carry
    lax.fori_loop(0, nrows // GATHER_UNROLL, body, 0)


def _row_gather_wait(src_hbm, dst, sem, nrows):
    pltpu.make_async_copy(src_hbm.at[pl.ds(0, nrows)], dst, sem).wait()


def _ffn_kernel(be_ref, tok_ref, tokn_ref, f_hbm, w1_ref, w3_ref, w2_ref, y_ref, xbuf, sem):
    i = pl.program_id(0)
    nblk = pl.num_programs(0)
    slot = lax.rem(i, 2)

    @pl.when(i == 0)
    def _():
        _row_gather_start(tok_ref, f_hbm, xbuf.at[0], sem.at[0], MOE_ROWS)

    @pl.when(i + 1 < nblk)
    def _():
        _row_gather_start(tokn_ref, f_hbm, xbuf.at[1 - slot], sem.at[1 - slot], MOE_ROWS)

    _row_gather_wait(f_hbm, xbuf.at[slot], sem.at[slot], MOE_ROWS)
    x = xbuf[slot].astype(BF16)
    a = _dot(x, w1_ref[0])
    hid = a * jax.nn.sigmoid(a) * _dot(x, w3_ref[0])
    y_ref[...] = _dot(hid.astype(BF16), w2_ref[0])


def _expert_ffn(f, buf_tok, blk_expert, w1, w3, w2):
    n_blocks = buf_tok.shape[0]
    mb = MOE_ROWS
    idx_spec = lambda off: pl.BlockSpec((1, 1, mb), lambda i, be: (jnp.minimum(i + off, n_blocks - 1), 0, 0),
                                        memory_space=pltpu.SMEM)
    return pl.pallas_call(
        _ffn_kernel,
        out_shape=jax.ShapeDtypeStruct((n_blocks * mb, D_MODEL), F32),
        grid_spec=pltpu.PrefetchScalarGridSpec(
            num_scalar_prefetch=1,
            grid=(n_blocks,),
            in_specs=[
                idx_spec(0), idx_spec(1),
                pl.BlockSpec(memory_space=pl.ANY),
                pl.BlockSpec((1, D_MODEL, D_FF_EXPERT), lambda i, be: (be[i], 0, 0)),
                pl.BlockSpec((1, D_MODEL, D_FF_EXPERT), lambda i, be: (be[i], 0, 0)),
                pl.BlockSpec((1, D_FF_EXPERT, D_MODEL), lambda i, be: (be[i], 0, 0)),
            ],
            out_specs=pl.BlockSpec((mb, D_MODEL), lambda i, be: (i, 0)),
            scratch_shapes=[pltpu.VMEM((2, mb, D_MODEL), F32), pltpu.SemaphoreType.DMA((2,))],
        ),
        compiler_params=pltpu.CompilerParams(dimension_semantics=("arbitrary",), vmem_limit_bytes=VMEM_LIMIT),
        name="moe_expert_ffn",
    )(blk_expert, buf_tok, buf_tok, f, w1, w3, w2)


def _combine_kernel(d_ref, dn_ref, h_ref, mod_ref, gate_ref, y_hbm, o_ref, rbuf, sem):
    i = pl.program_id(0)
    nsteps = pl.num_programs(0)
    slot = lax.rem(i, 2)
    tl = ROW_TILE

    @pl.when(i == 0)
    def _():
        _row_gather_start(d_ref, y_hbm, rbuf.at[0], sem.at[0], 2 * tl)

    @pl.when(i + 1 < nsteps)
    def _():
        _row_gather_start(dn_ref, y_hbm, rbuf.at[1 - slot], sem.at[1 - slot], 2 * tl)

    _row_gather_wait(y_hbm, rbuf.at[slot], sem.at[slot], 2 * tl)
    g = gate_ref[...]
    out = g[:, 0:1] * rbuf[slot, 0:tl, :] + g[:, 1:2] * rbuf[slot, tl:2 * tl, :]
    o_ref[...] = h_ref[...] + mod_ref[0, 0][5:6] * out


def _combine(hflat, modall, gate, dest, yb, tiles_per_sample, nct):
    n = hflat.shape[0]
    tl = ROW_TILE
    nsteps = n // tl
    d2 = dest.reshape(nsteps, tl, 2).transpose(0, 2, 1).reshape(nsteps, 1, 2 * tl)
    idx_spec = lambda off: pl.BlockSpec((1, 1, 2 * tl), lambda i: (jnp.minimum(i + off, nsteps - 1), 0, 0),
                                        memory_space=pltpu.SMEM)
    row = lambda w: pl.BlockSpec((tl, w), lambda i: (i, 0))

    def mod_map(i):
        return (i // tiles_per_sample, jnp.where(i % tiles_per_sample >= nct, 1, 0), 0, 0)

    return pl.pallas_call(
        _combine_kernel,
        out_shape=jax.ShapeDtypeStruct((n, D_MODEL), F32),
        grid=(nsteps,),
        in_specs=[
            idx_spec(0), idx_spec(1), row(D_MODEL),
            pl.BlockSpec((1, 1, 8, D_MODEL), mod_map),
            row(LOGIT_W), pl.BlockSpec(memory_space=pl.ANY),
        ],
        out_specs=row(D_MODEL),
        scratch_shapes=[pltpu.VMEM((2, 2 * tl, D_MODEL), F32), pltpu.SemaphoreType.DMA((2,))],
        compiler_params=pltpu.CompilerParams(dimension_semantics=("arbitrary",), vmem_limit_bytes=VMEM_LIMIT),
        name="moe_combine",
    )(d2, d2, hflat, modall, gate, yb)


def _rope_tables(lat_len, ctx_len):
    pos = jnp.arange(lat_len, dtype=jnp.int32)
    n_freq = HEAD_DIM // 4
    inv = ROPE_BASE ** (-jnp.arange(n_freq, dtype=F32) / n_freq)
    ang_r = (pos // GRID_W).astype(F32)[:, None] * inv
    ang_c = (pos % GRID_W).astype(F32)[:, None] * inv
    zero = jnp.zeros_like(ang_r)
    cos = jnp.concatenate([jnp.cos(ang_r)] * 2 + [jnp.cos(ang_c)] * 2, axis=-1)
    sin_up = jnp.concatenate([-jnp.sin(ang_r), zero, -jnp.sin(ang_c), zero], axis=-1)
    sin_dn = jnp.concatenate([zero, jnp.sin(ang_r), zero, jnp.sin(ang_c)], axis=-1)

    def full(tab, ctx_val):
        tab = jnp.tile(tab, (1, N_HEADS))
        return jnp.concatenate([jnp.full((ctx_len, BRANCH_W), ctx_val, F32), tab], axis=0)

    return full(cos, 1.0), full(sin_up, 0.0), full(sin_dn, 0.0)


def kernel(x, c, ctx, c_ctx, mod_w, mod_b, norm1_g, norm2_g, w_in, w_branch, w_out, ret_log_decay,
           na_qk_gain, na_rpb, s5_lambda_re, s5_lambda_im, s5_log_step, s5_b_re, s5_b_im, s5_c_re,
           s5_c_im, s5_d, s5_glu_w, s5_glu_b, gqa_qk_gain, gqa_sink, router_w1, router_b1, router_w2,
           router_b2, exp_w1, exp_w3, exp_w2):
    depth = mod_w.shape[0]
    bsz, lat_len, _ = x.shape
    ctx_len = ctx.shape[1]
    t = ctx_len + lat_len
    nct = ctx_len // ROW_TILE
    assert ctx_len % ROW_TILE == 0 and lat_len % (NA_QROWS * GRID_W) == 0 and ctx_len % RET_CHUNK == 0
    assert lat_len // GRID_W >= NA_KROWS and lat_len >= GQA_KT

    hall = jnp.concatenate([ctx, x], axis=1)
    modall = _modulation(c, c_ctx, mod_w, mod_b)
    rope_tabs = _rope_tables(lat_len, ctx_len)
    ones_bd = jnp.asarray(np.kron(np.eye(N_HEADS), np.full((HEAD_DIM, HEAD_DIM), 1.0 / HEAD_DIM)), BF16)

    for i in range(depth):
        g1 = norm1_g[i].reshape(1, D_MODEL)
        g2 = norm2_g[i].reshape(1, D_MODEL)
        w_mix = w_in[i, :, :MIX_COLS].astype(BF16)
        w_gate = w_in[i, :, MIX_COLS:].astype(BF16)
        gains = jnp.concatenate([jnp.tile(na_qk_gain[i], (1, N_HEADS)), jnp.tile(gqa_qk_gain[i], (1, N_HEADS))], 0)
        ret, na, u, gqa = _inproj(hall, g1, modall[i], w_mix, rope_tabs, gains, ones_bd, nct)

        dmask, rtabs, cdec = _ret_consts(ret_log_decay[i])
        ya = _retention(ret, dmask, rtabs, cdec, ctx_len // RET_CHUNK)
        yb = _neighborhood(na, _na_bias_tables(na_rpb[i], lat_len // GRID_W), ctx_len)
        s5w, s5cp, s5scan = _s5_consts(s5_lambda_re[i], s5_lambda_im[i], s5_log_step[i], s5_b_re[i], s5_b_im[i],
                                       s5_c_re[i], s5_c_im[i])
        ys = _s5_core(u, s5w, s5cp, s5scan, ctx_len)
        yd = _window_gqa(gqa, gqa_sink[i], ctx_len)

        s5p = jnp.zeros((8, BRANCH_W), F32).at[0].set(s5_d[i]).at[1].set(s5_glu_b[i])
        rw = jnp.zeros((D_MODEL, LOGIT_W), F32).at[:, :MOE_GROUPS].set(router_w1[i])
        rw = rw.at[:, MOE_GROUPS:MOE_GROUPS + N_EXPERTS].set(router_w2[i])
        rwh, rwl = _split_bf16(rw)
        rbias = jnp.zeros((1, LOGIT_W), F32).at[0, :MOE_GROUPS].set(router_b1[i])
        rbias = rbias.at[0, MOE_GROUPS:MOE_GROUPS + N_EXPERTS].set(router_b2[i])
        hall, f, logits = _merge(hall, g1, modall[i], ya, yb, ys, u, yd, s5p, s5_glu_w[i].astype(BF16), w_gate,
                                 w_branch[i].astype(BF16), w_out[i].astype(BF16), g2, rwh, rwl, rbias, nct)

        n = bsz * t
        idx, gate = _route(logits.reshape(n, LOGIT_W))
        dest, buf_tok, blk_expert = _dispatch_plan(idx[:, :2])
        yexp = _expert_ffn(f.reshape(n, D_MODEL), buf_tok, blk_expert, exp_w1[i].astype(BF16),
                           exp_w3[i].astype(BF16), exp_w2[i].astype(BF16))
        hall = _combine(hall.reshape(n, D_MODEL), modall[i], gate, dest, yexp, t // ROW_TILE, nct)
        hall = hall.reshape(bsz, t, D_MODEL)

    return hall[:, ctx_len:]
```

```python
import functools
import math

import numpy as np
import jax
import jax.numpy as jnp
from jax import lax
from jax.experimental import pallas as pl
from jax.experimental.pallas import tpu as pltpu

F32 = jnp.float32
BF16 = jnp.bfloat16

D_MODEL = 1024
GRID_W = 64
HEAD_DIM = 64
N_BRANCH = 4
BRANCH_W = D_MODEL // 4
N_HEADS = BRANCH_W // HEAD_DIM
RET_CHUNK = 128
NA_KH = 8
NA_KW = 16
NA_QROWS = 4
NA_KROWS = NA_QROWS + NA_KH
S5_GROUP_CH = 16
S5_GROUPS = BRANCH_W // S5_GROUP_CH
S5_STATE = 64
S5_CHUNK = 16
GQA_KV_HEADS = 2
GQA_KV_W = GQA_KV_HEADS * HEAD_DIM
WIN = 128
MOE_GROUPS = 4
EXPERTS_PER_GROUP = 8
N_EXPERTS = MOE_GROUPS * EXPERTS_PER_GROUP
D_FF_EXPERT = D_MODEL // 2
ROPE_BASE = 10000.0
EPS = 1e-6
NEG = -1e30
MIX_COLS = 9 * BRANCH_W + 2 * GQA_KV_W

ROW_TILE = 256
MOE_ROWS = 256
LOGIT_W = 128
VMEM_LIMIT = 56 * 1024 * 1024


def _const_spec(shape):
    nd = len(shape)
    return pl.BlockSpec(shape, lambda *_: (0,) * nd)


def _dot(a, b):
    return jnp.dot(a, b, preferred_element_type=F32)


def _dot_nt(a, b):
    return lax.dot_general(a, b, (((1,), (1,)), ((), ())), preferred_element_type=F32)


def _dot_tn(a, b):
    return lax.dot_general(a, b, (((0,), (0,)), ((), ())), preferred_element_type=F32)


def _split_bf16(x):
    hi = x.astype(BF16)
    lo = (x - hi.astype(F32)).astype(BF16)
    return hi, lo


def _norm_modulate(x, gain, mod, shift_row, scale_row):
    ms = jnp.mean(x * x, axis=-1, keepdims=True)
    a = x * lax.rsqrt(ms + EPS) * gain
    return a * (1.0 + mod[scale_row:scale_row + 1]) + mod[shift_row:shift_row + 1]


def _mod_kernel(c_ref, w_ref, b_ref, o_ref):
    c = c_ref[...]
    s = c * jax.nn.sigmoid(c)
    o_ref[0] = _dot(s.astype(BF16), w_ref[0].astype(BF16)) + b_ref[0]


def _modulation(c, c_ctx, mod_w, mod_b):
    depth = mod_w.shape[0]
    b = c.shape[0]
    rows = -(-(b + 1) // 8) * 8
    cs = jnp.zeros((rows, D_MODEL), F32).at[:b].set(c).at[b].set(c_ctx)
    out = pl.pallas_call(
        _mod_kernel,
        out_shape=jax.ShapeDtypeStruct((depth, rows, 6 * D_MODEL), F32),
        grid=(depth, 6),
        in_specs=[
            _const_spec((rows, D_MODEL)),
            pl.BlockSpec((1, D_MODEL, D_MODEL), lambda l, j: (l, 0, j)),
            pl.BlockSpec((1, 1, D_MODEL), lambda l, j: (l, 0, j)),
        ],
        out_specs=pl.BlockSpec((1, rows, D_MODEL), lambda l, j: (l, 0, j)),
        compiler_params=pltpu.CompilerParams(dimension_semantics=("parallel", "parallel")),
        name="adaln_mod",
    )(cs, mod_w, mod_b.reshape(depth, 1, 6 * D_MODEL))
    lat = out[:, :b].reshape(depth, b, 6, D_MODEL)
    ctx = jnp.broadcast_to(out[:, b].reshape(depth, 1, 6, D_MODEL), (depth, b, 6, D_MODEL))
    modall = jnp.stack([ctx, lat], axis=2)
    return jnp.pad(modall, ((0, 0), (0, 0), (0, 0), (0, 2), (0, 0)))


def _head_mean_sq(x, ones_bd):
    hi, lo = _split_bf16(x * x)
    return _dot(hi, ones_bd) + _dot(lo, ones_bd)


def _rope(x, cos, sin_up, sin_dn):
    w = x.shape[-1]
    return x * cos + pltpu.roll(x, w - 16, 1) * sin_up + pltpu.roll(x, 16, 1) * sin_dn


def _inproj_kernel(h_ref, g1_ref, mod_ref, w_ref, cos_ref, sup_ref, sdn_ref, gains_ref, ones_ref,
                   ret_ref, na_ref, u_ref, gqa_ref):
    a = _norm_modulate(h_ref[0], g1_ref[...], mod_ref[0, 0], 0, 1)
    p = _dot(a.astype(BF16), w_ref[...])
    cos, sup, sdn = cos_ref[...], sup_ref[...], sdn_ref[...]
    ones_bd = ones_ref[...]
    scale = HEAD_DIM ** -0.5
    bw = BRANCH_W

    def col(i):
        return p[:, i * bw:(i + 1) * bw]

    def head_rms(x, gain):
        w = x.shape[-1]
        ms = _head_mean_sq(x, ones_bd[:w, :w])
        return x * lax.rsqrt(ms + EPS) * gain

    ret_ref[0, :, 0 * bw:1 * bw] = (_rope(col(0), cos, sup, sdn) * scale).astype(BF16)
    ret_ref[0, :, 1 * bw:2 * bw] = _rope(col(1), cos, sup, sdn).astype(BF16)
    ret_ref[0, :, 2 * bw:3 * bw] = col(2).astype(BF16)
    ret_ref[0, :, 3 * bw:4 * bw] = col(3).astype(BF16)
    na_ref[0, :, 0 * bw:1 * bw] = (head_rms(col(4), gains_ref[0:1]) * scale).astype(BF16)
    na_ref[0, :, 1 * bw:2 * bw] = head_rms(col(5), gains_ref[1:2]).astype(BF16)
    na_ref[0, :, 2 * bw:3 * bw] = col(6).astype(BF16)
    u_ref[0] = col(7)
    gq = _rope(head_rms(col(8), gains_ref[2:3]), cos, sup, sdn) * scale
    gqa_ref[0, :, 0:bw] = gq.astype(BF16)
    kw = GQA_KV_W
    gk = head_rms(p[:, 9 * bw:9 * bw + kw], gains_ref[3:4, :kw])
    gk = _rope(gk, cos[:, :kw], sup[:, :kw], sdn[:, :kw])
    gqa_ref[0, :, bw:bw + kw] = gk.astype(BF16)
    gqa_ref[0, :, bw + kw:bw + 2 * kw] = p[:, 9 * bw + kw:9 * bw + 2 * kw].astype(BF16)


def _mod_spec(nct):
    return pl.BlockSpec((1, 1, 8, D_MODEL), lambda b, t: (b, jnp.where(t >= nct, 1, 0), 0, 0))


def _inproj(hall, g1, modall, w_mix, rope_tabs, gains, ones_bd, nct):
    b, t, _ = hall.shape
    tl = ROW_TILE
    row = lambda w: pl.BlockSpec((1, tl, w), lambda b, t: (b, t, 0))
    tab = pl.BlockSpec((tl, BRANCH_W), lambda b, t: (t, 0))
    return pl.pallas_call(
        _inproj_kernel,
        out_shape=(
            jax.ShapeDtypeStruct((b, t, 4 * BRANCH_W), BF16),
            jax.ShapeDtypeStruct((b, t, 3 * BRANCH_W), BF16),
            jax.ShapeDtypeStruct((b, t, BRANCH_W), F32),
            jax.ShapeDtypeStruct((b, t, BRANCH_W + 2 * GQA_KV_W), BF16),
        ),
        grid=(b, t // tl),
        in_specs=[
            row(D_MODEL), _const_spec((1, D_MODEL)), _mod_spec(nct), _const_spec((D_MODEL, MIX_COLS)),
            tab, tab, tab, _const_spec((4, BRANCH_W)), _const_spec((BRANCH_W, BRANCH_W)),
        ],
        out_specs=(row(4 * BRANCH_W), row(3 * BRANCH_W), row(BRANCH_W), row(BRANCH_W + 2 * GQA_KV_W)),
        compiler_params=pltpu.CompilerParams(dimension_semantics=("parallel", "parallel"),
                                             vmem_limit_bytes=VMEM_LIMIT),
        name="inproj",
    )(hall, g1, modall, w_mix, *rope_tabs, gains, ones_bd)


def _ret_consts(log_decay):
    lg = -jnp.exp(log_decay.astype(F32))
    idx = jnp.arange(RET_CHUNK, dtype=F32)
    diff = idx[:, None] - idx[None, :]
    fwd = jnp.where(diff > 0, jnp.exp(jnp.maximum(diff, 0.0)[None] * lg[0][:, None, None]), 0.0)
    bwd = jnp.where(diff < 0, jnp.exp(jnp.maximum(-diff, 0.0)[None] * lg[1][:, None, None]), 0.0)
    dmask = fwd + bwd + 2.0 * jnp.eye(RET_CHUNK, dtype=F32)[None]
    tabs = jnp.stack([
        jnp.exp((RET_CHUNK - 1 - idx)[None, :] * lg[0][:, None]),
        jnp.exp(idx[None, :] * lg[1][:, None]),
        jnp.exp((idx + 1.0)[None, :] * lg[0][:, None]),
        jnp.exp((RET_CHUNK - idx)[None, :] * lg[1][:, None]),
    ])
    tabs = jnp.repeat(tabs.transpose(0, 2, 1), HEAD_DIM, axis=-1)
    cdec = jnp.exp(RET_CHUNK * lg).reshape(2 * N_HEADS)
    return dmask, tabs, cdec


def _ret_bwd_chunk(i, ncc, nch):
    return jnp.where(i < ncc, ncc - 1 - i, nch - 1 - (i - ncc))


def _ret_kernel(cdec_ref, q_ref, k_ref, v_ref, g_ref, dmask_ref, tab_ref, o_ref, sf, sb, sbprev, *, ncc):
    ph = pl.program_id(1)
    i = pl.program_id(2)
    nch = pl.num_programs(2)
    hd = HEAD_DIM

    @pl.when(jnp.logical_and(ph == 0, i == 0))
    def _():
        sb[...] = jnp.zeros_like(sb)

    @pl.when(jnp.logical_and(ph == 1, i == 0))
    def _():
        sf[...] = jnp.zeros_like(sf)

    k = k_ref[0]
    v = v_ref[0]

    @pl.when(ph == 0)
    def _():
        n = _ret_bwd_chunk(i, ncc, nch)
        kd = (k.astype(F32) * tab_ref[1]).astype(BF16)
        kv = _dot_tn(kd, v)
        for h in range(N_HEADS):
            sbprev[n, h] = sb[h]
            sb[h] = cdec_ref[N_HEADS + h] * sb[h] + kv[h * hd:(h + 1) * hd, h * hd:(h + 1) * hd]

    @pl.when(ph == 1)
    def _():
        q = q_ref[0]
        qf32 = q.astype(F32)
        qf = (qf32 * tab_ref[2]).astype(BF16)
        qb = (qf32 * tab_ref[3]).astype(BF16)
        kd = (k.astype(F32) * tab_ref[0]).astype(BF16)
        kv = _dot_tn(kd, v)
        outs = []
        for h in range(N_HEADS):
            hs = slice(h * hd, (h + 1) * hd)
            s = _dot_nt(q[:, hs], k[:, hs]) * dmask_ref[h]
            o = _dot(s.astype(BF16), v[:, hs])
            o = o + _dot(qf[:, hs], sf[h].astype(BF16)) + _dot(qb[:, hs], sbprev[i, h].astype(BF16))
            sf[h] = cdec_ref[h] * sf[h] + kv[hs, hs]
            outs.append(o * lax.rsqrt(jnp.mean(o * o, axis=-1, keepdims=True) + EPS))
        g = g_ref[0].astype(F32)
        o_ref[0] = (jnp.concatenate(outs, axis=-1) * (g * jax.nn.sigmoid(g))).astype(BF16)


def _retention(ret, dmask, tabs, cdec, ncc):
    b, t, _ = ret.shape
    nch = t // RET_CHUNK
    bw = BRANCH_W

    def kv_map(col):
        return lambda b, ph, i, *_: (b, jnp.where(ph == 0, _ret_bwd_chunk(i, ncc, nch), i), col)

    def q_map(col):
        return lambda b, ph, i, *_: (b, jnp.where(ph == 0, 0, i), col)

    blk = (1, RET_CHUNK, bw)
    return pl.pallas_call(
        functools.partial(_ret_kernel, ncc=ncc),
        out_shape=jax.ShapeDtypeStruct((b, t, bw), BF16),
        grid_spec=pltpu.PrefetchScalarGridSpec(
            num_scalar_prefetch=1,
            grid=(b, 2, nch),
            in_specs=[
                pl.BlockSpec(blk, q_map(0)), pl.BlockSpec(blk, kv_map(1)), pl.BlockSpec(blk, kv_map(2)),
                pl.BlockSpec(blk, q_map(3)),
                _const_spec((N_HEADS, RET_CHUNK, RET_CHUNK)), _const_spec((4, RET_CHUNK, bw)),
            ],
            out_specs=pl.BlockSpec(blk, q_map(0)),
            scratch_shapes=[
                pltpu.VMEM((N_HEADS, HEAD_DIM, HEAD_DIM), F32),
                pltpu.VMEM((N_HEADS, HEAD_DIM, HEAD_DIM), F32),
                pltpu.VMEM((nch, N_HEADS, HEAD_DIM, HEAD_DIM), F32),
            ],
        ),
        compiler_params=pltpu.CompilerParams(dimension_semantics=("parallel", "arbitrary", "arbitrary"),
                                             vmem_limit_bytes=VMEM_LIMIT),
        name="retention",
    )(cdec, ret, ret, ret, ret, dmask, tabs)


def _na_bias_tables(rpb, rows):
    nrb = rows // NA_QROWS
    qi = np.arange(NA_QROWS)
    kl = np.arange(NA_KROWS)
    ridx, rvalid = [], []
    for rb in (0, 1, nrb - 1):
        u0 = int(np.clip(NA_QROWS * rb - NA_KH // 2, 0, rows - NA_KROWS))
        r = NA_QROWS * rb + qi[:, None]
        kr = u0 + kl[None, :]
        r0 = np.clip(r - NA_KH // 2, 0, rows - NA_KH)
        rvalid.append((kr >= r0) & (kr < r0 + NA_KH))
        ridx.append(np.clip(kr - r + NA_KH - 1, 0, 2 * NA_KH - 2))
    ridx, rvalid = np.stack(ridx), np.stack(rvalid)
    qc = np.arange(GRID_W)[:, None]
    kc = np.arange(GRID_W)[None, :]
    ws = np.clip(qc - NA_KW // 2, 0, GRID_W - NA_KW)
    cvalid = (kc >= ws) & (kc < ws + NA_KW)
    cidx = np.clip(kc - qc + NA_KW - 1, 0, 2 * NA_KW - 2)
    row_sel = jnp.asarray(ridx[..., None] == np.arange(2 * NA_KH - 1), F32)
    col_sel = jnp.asarray(cidx[..., None] == np.arange(2 * NA_KW - 1), F32)
    hp = lax.Precision.HIGHEST
    rows_b = jnp.einsum('tikr,hrc->thikc', row_sel, rpb.astype(F32), precision=hp)
    bias = jnp.einsum('thikc,qzc->thiqkz', rows_b, col_sel, precision=hp)
    valid = rvalid[:, None, :, None, :, None] & cvalid[None, None, None, :, None, :]
    bias = jnp.where(jnp.asarray(valid), bias, NEG)
    return bias.reshape(3, N_HEADS, NA_QROWS * GRID_W, NA_KROWS * GRID_W)


def _softmax_pv(parts, sink=None):
    m = functools.reduce(jnp.maximum, [jnp.max(s, axis=-1, keepdims=True) for s, _ in parts])
    if sink is not None:
        m = jnp.maximum(m, sink)
    den = None
    acc = None
    for s, v in parts:
        p = jnp.exp(s - m)
        d = jnp.sum(p, axis=-1, keepdims=True)
        a = _dot(p.astype(BF16), v)
        den = d if den is None else den + d
        acc = a if acc is None else acc + a
    if sink is not None:
        den = den + jnp.exp(sink - m)
    return acc / den


def _na_kernel(q_ref, k_ref, v_ref, bias_ref, o_ref, *, nct, ctx_len, rows):
    j = pl.program_id(1)
    hd = HEAD_DIM
    q = q_ref[0]
    kc = k_ref[0, 0:ctx_len, :]
    vc = v_ref[0, 0:ctx_len, :]

    @pl.when(j < nct)
    def _():
        outs = []
        for h in range(N_HEADS):
            hs = slice(h * hd, (h + 1) * hd)
            outs.append(_softmax_pv([(_dot_nt(q[:, hs], kc[:, hs]), vc[:, hs])]))
        o_ref[0] = jnp.concatenate(outs, axis=-1).astype(BF16)

    @pl.when(j >= nct)
    def _():
        rb = j - nct
        u0 = jnp.clip(NA_QROWS * rb - NA_KH // 2, 0, rows - NA_KROWS)
        start = pl.multiple_of(ctx_len + u0 * GRID_W, GRID_W)
        kw = k_ref[0, pl.ds(start, NA_KROWS * GRID_W), :]
        vw = v_ref[0, pl.ds(start, NA_KROWS * GRID_W), :]
        outs = []
        for h in range(N_HEADS):
            hs = slice(h * hd, (h + 1) * hd)
            s_nb = _dot_nt(q[:, hs], kw[:, hs]) + bias_ref[0, h]
            s_cx = _dot_nt(q[:, hs], kc[:, hs])
            outs.append(_softmax_pv([(s_nb, vw[:, hs]), (s_cx, vc[:, hs])]))
        o_ref[0] = jnp.concatenate(outs, axis=-1).astype(BF16)


def _neighborhood(na, bias, ctx_len):
    b, t, _ = na.shape
    bw = BRANCH_W
    qt = NA_QROWS * GRID_W
    nct = ctx_len // qt
    nsteps = t // qt
    nrb = nsteps - nct
    rows = nrb * NA_QROWS

    def bias_map(b, j):
        rb = j - nct
        return (jnp.where(rb <= 0, 0, jnp.where(rb == nrb - 1, 2, 1)), 0, 0, 0)

    return pl.pallas_call(
        functools.partial(_na_kernel, nct=nct, ctx_len=ctx_len, rows=rows),
        out_shape=jax.ShapeDtypeStruct((b, t, bw), BF16),
        grid=(b, nsteps),
        in_specs=[
            pl.BlockSpec((1, qt, bw), lambda b, j: (b, j, 0)),
            pl.BlockSpec((1, t, bw), lambda b, j: (b, 0, 1)),
            pl.BlockSpec((1, t, bw), lambda b, j: (b, 0, 2)),
            pl.BlockSpec((1, N_HEADS, qt, NA_KROWS * GRID_W), bias_map),
        ],
        out_specs=pl.BlockSpec((1, qt, bw), lambda b, j: (b, j, 0)),
        compiler_params=pltpu.CompilerParams(dimension_semantics=("parallel", "arbitrary"),
                                             vmem_limit_bytes=VMEM_LIMIT),
        name="neighborhood_attn",
    )(na, na, na, bias)


GQA_QT = 256
GQA_KT = GQA_QT + 2 * WIN


def _gqa_kernel(sink_ref, q_ref, kv_ref, o_ref, *, nct, ctx_len, lat_len):
    j = pl.program_id(1)
    hd = HEAD_DIM
    grp = N_HEADS // GQA_KV_HEADS
    q = q_ref[0]
    kvc = kv_ref[0, 0:ctx_len, :]

    def stacked_q(kvh):
        return jnp.concatenate([q[:, (kvh * grp + g) * hd:(kvh * grp + g + 1) * hd] for g in range(grp)], axis=0)

    def sink_col(kvh):
        return jnp.concatenate([jnp.full((GQA_QT, 1), sink_ref[kvh * grp + g], F32) for g in range(grp)], axis=0)

    def emit(results):
        cols = [results[kvh][g * GQA_QT:(g + 1) * GQA_QT] for kvh in range(GQA_KV_HEADS) for g in range(grp)]
        o_ref[0] = jnp.concatenate(cols, axis=-1).astype(BF16)

    @pl.when(j < nct)
    def _():
        res = []
        for kvh in range(GQA_KV_HEADS):
            kc = kvc[:, kvh * hd:(kvh + 1) * hd]
            vc = kvc[:, GQA_KV_W + kvh * hd:GQA_KV_W + (kvh + 1) * hd]
            res.append(_softmax_pv([(_dot_nt(stacked_q(kvh), kc), vc)], sink=sink_col(kvh)))
        emit(res)

    @pl.when(j >= nct)
    def _():
        q0 = (j - nct) * GQA_QT
        k0 = jnp.clip(q0 - WIN, 0, lat_len - GQA_KT)
        kvw = kv_ref[0, pl.ds(pl.multiple_of(ctx_len + k0, WIN), GQA_KT), :]
        qpos = q0 + (lax.broadcasted_iota(jnp.int32, (grp * GQA_QT, GQA_KT), 0) & (GQA_QT - 1))
        kpos = k0 + lax.broadcasted_iota(jnp.int32, (grp * GQA_QT, GQA_KT), 1)
        band = jnp.abs(qpos - kpos) <= WIN
        res = []
        for kvh in range(GQA_KV_HEADS):
            ks = slice(kvh * hd, (kvh + 1) * hd)
            vs = slice(GQA_KV_W + kvh * hd, GQA_KV_W + (kvh + 1) * hd)
            q2 = stacked_q(kvh)
            s_w = jnp.where(band, _dot_nt(q2, kvw[:, ks]), NEG)
            s_c = _dot_nt(q2, kvc[:, ks])
            res.append(_softmax_pv([(s_w, kvw[:, vs]), (s_c, kvc[:, vs])], sink=sink_col(kvh)))
        emit(res)


def _window_gqa(gqa, sink, ctx_len):
    b, t, _ = gqa.shape
    bw = BRANCH_W
    nct = ctx_len // GQA_QT
    return pl.pallas_call(
        functools.partial(_gqa_kernel, nct=nct, ctx_len=ctx_len, lat_len=t - ctx_len),
        out_shape=jax.ShapeDtypeStruct((b, t, bw), BF16),
        grid_spec=pltpu.PrefetchScalarGridSpec(
            num_scalar_prefetch=1,
            grid=(b, t // GQA_QT),
            in_specs=[
                pl.BlockSpec((1, GQA_QT, bw), lambda b, j, *_: (b, j, 0)),
                pl.BlockSpec((1, t, bw), lambda b, j, *_: (b, 0, 1)),
            ],
            out_specs=pl.BlockSpec((1, GQA_QT, bw), lambda b, j, *_: (b, j, 0)),
        ),
        compiler_params=pltpu.CompilerParams(dimension_semantics=("parallel", "arbitrary"),
                                             vmem_limit_bytes=VMEM_LIMIT),
        name="window_gqa",
    )(sink.astype(F32), gqa, gqa)


def _cmul(ar, ai, br, bi):
    return ar * br - ai * bi, ar * bi + ai * br


def _s5_consts(lam_re, lam_im, log_step, b_re, b_im, c_re, c_im):
    f32 = F32
    tc = S5_CHUNK
    cre, cim = c_re.astype(f32), c_im.astype(f32)
    bre, bim = b_re.astype(f32), b_im.astype(f32)
    kern, inj, cpow, scan = [], [], [], []
    for d in range(2):
        lre, lim, dt = lam_re[d].astype(f32), lam_im[d].astype(f32), jnp.exp(log_step[d].astype(f32))[:, None]
        mag = jnp.exp(lre * dt)
        ar, ai = mag * jnp.cos(lim * dt), mag * jnp.sin(lim * dt)
        den = lre * lre + lim * lim
        nr = ar - 1.0
        coef_re = (nr * lre + ai * lim) / den
        coef_im = (ai * lre - nr * lim) / den
        bbr = coef_re[..., None] * bre - coef_im[..., None] * bim
        bbi = coef_re[..., None] * bim + coef_im[..., None] * bre
        pr, pi = [jnp.ones_like(ar)], [jnp.zeros_like(ar)]
        for _ in range(tc):
            nr_, ni_ = _cmul(pr[-1], pi[-1], ar, ai)
            pr.append(nr_)
            pi.append(ni_)
        pr, pi = jnp.stack(pr), jnp.stack(pi)
        lbr, lbi = _cmul(pr[:tc, :, :, None], pi[:tc, :, :, None], bbr[None], bbi[None])
        kern.append(jnp.einsum('gcp,tgpd->tgcd', cre, lbr, precision=lax.Precision.HIGHEST)
                    - jnp.einsum('gcp,tgpd->tgcd', cim, lbi, precision=lax.Precision.HIGHEST))
        inj.append((lbr[::-1], lbi[::-1]) if d == 0 else (lbr, lbi))
        pwr, pwi = (pr[1:], pi[1:]) if d == 0 else (pr[1:][::-1], pi[1:][::-1])
        dr, di = _cmul(cre[None], cim[None], pwr[:, :, None, :], pwi[:, :, None, :])
        cpow.append((dr, -di))
        scan.append((pr[tc], pi[tc]))
    g, gc, p = S5_GROUPS, S5_GROUP_CH, S5_STATE
    lag = np.arange(tc)[None, :] - np.arange(tc)[:, None]
    taus = np.arange(tc)[:, None, None]
    sel_f = jnp.asarray(lag[None] == taus, F32)
    sel_b = jnp.asarray(-lag[None] == taus, F32)
    hp = lax.Precision.HIGHEST
    toep = (jnp.einsum('xst,xgcd->gsdtc', sel_f, kern[0], precision=hp)
            + jnp.einsum('xst,xgcd->gsdtc', sel_b, kern[1], precision=hp))
    toep = toep.reshape(g, tc * gc, tc * gc)
    injm = jnp.concatenate([x.transpose(1, 0, 3, 2).reshape(g, tc * gc, p)
                            for x in (inj[0][0], inj[0][1], inj[1][0], inj[1][1])], axis=-1)
    w = jnp.concatenate([toep, injm], axis=-1)
    cp = jnp.concatenate([x.transpose(1, 3, 0, 2).reshape(g, p, tc * gc)
                          for x in (cpow[0][0], cpow[0][1], cpow[1][0], cpow[1][1])], axis=1)
    rows = []
    for d in range(2):
        sr, si = scan[d]
        rows += [jnp.concatenate([sr, sr], -1), jnp.concatenate([-si, si], -1)]
    scan_c = jnp.stack(rows, axis=1)
    scan_c = jnp.pad(scan_c, ((0, 0), (0, 4), (0, 0)))
    return w.astype(BF16), cp.astype(BF16), scan_c


def _s5_kernel(u_ref, w_ref, cp_ref, sc_ref, y_ref, inj, injs, xs, *, bsz, nch, ncc):
    p2 = 2 * S5_STATE
    r = _dot(u_ref[0], w_ref[0])
    y_ref[0] = r[:, :BRANCH_W]
    rf = r[:, BRANCH_W:BRANCH_W + p2]
    rb = r[:, BRANCH_W + p2:]
    inj[:, 0:p2] = rf
    inj[:, p2:] = rb
    injs[:, 0:p2] = pltpu.roll(rf, S5_STATE, 1)
    injs[:, p2:] = pltpu.roll(rb, S5_STATE, 1)
    a_f, s_f, a_b, s_b = sc_ref[0, 0:1], sc_ref[0, 1:2], sc_ref[0, 2:3], sc_ref[0, 3:4]

    def body(n, carry):
        xf, xfs, xb, xbs = carry
        rowf = pl.multiple_of(n * bsz, bsz)
        nb = jnp.where(n < ncc, ncc - 1 - n, nch - 1 - (n - ncc))
        rowb = pl.multiple_of(nb * bsz, bsz)
        xs[pl.ds(rowf, bsz), 0:p2] = xf
        xs[pl.ds(rowb, bsz), p2:] = xb
        nxf = a_f * xf + s_f * xfs + inj[pl.ds(rowf, bsz), 0:p2]
        nxfs = a_f * xfs - s_f * xf + injs[pl.ds(rowf, bsz), 0:p2]
        nxb = a_b * xb + s_b * xbs + inj[pl.ds(rowb, bsz), p2:]
        nxbs = a_b * xbs - s_b * xb + injs[pl.ds(rowb, bsz), p2:]
        return nxf, nxfs, nxb, nxbs

    z = jnp.zeros((bsz, p2), F32)
    lax.fori_loop(0, nch, body, (z, z, z, z))
    y_ref[0] += _dot(xs[...].astype(BF16), cp_ref[0])


def _s5_core(u, w, cp, scan_c, ctx_len):
    b, t, bw = u.shape
    g, gc, tc = S5_GROUPS, S5_GROUP_CH, S5_CHUNK
    nch = t // tc
    rows = nch * b
    ug = u.reshape(b, nch, tc, g, gc).transpose(3, 1, 0, 2, 4).reshape(g, rows, tc * gc).astype(BF16)
    y = pl.pallas_call(
        functools.partial(_s5_kernel, bsz=b, nch=nch, ncc=ctx_len // tc),
        out_shape=jax.ShapeDtypeStruct((g, rows, bw), F32),
        grid=(g,),
        in_specs=[
            pl.BlockSpec((1, rows, bw), lambda i: (i, 0, 0)),
            pl.BlockSpec((1, bw, bw + 4 * S5_STATE), lambda i: (i, 0, 0)),
            pl.BlockSpec((1, 4 * S5_STATE, bw), lambda i: (i, 0, 0)),
            pl.BlockSpec((1, 8, 2 * S5_STATE), lambda i: (i, 0, 0)),
        ],
        out_specs=pl.BlockSpec((1, rows, bw), lambda i: (i, 0, 0)),
        scratch_shapes=[pltpu.VMEM((rows, 4 * S5_STATE), F32)] * 3,
        compiler_params=pltpu.CompilerParams(dimension_semantics=("parallel",), vmem_limit_bytes=VMEM_LIMIT),
        name="s5_chunked",
    )(ug, w, cp, scan_c)
    return y.reshape(g, nch, b, tc, gc).transpose(2, 1, 3, 0, 4).reshape(b, t, bw)


def _merge_kernel(h_ref, g1_ref, mod_ref, ya_ref, yb_ref, ys_ref, u_ref, yd_ref, s5p_ref, gluw_ref,
                  wg_ref, wb_ref, wo_ref, g2_ref, rwh_ref, rwl_ref, rb_ref,
                  hout_ref, f_ref, lg_ref):
    x = h_ref[0]
    mod = mod_ref[0, 0]
    a = _norm_modulate(x, g1_ref[...], mod, 0, 1)
    gates = _dot(a.astype(BF16), wg_ref[...])
    z = ys_ref[0] + s5p_ref[0:1] * u_ref[0]
    z = jax.nn.gelu(z)
    yc = z * jax.nn.sigmoid(_dot(z.astype(BF16), gluw_ref[...]) + s5p_ref[1:2])
    ys = (ya_ref[0], yb_ref[0], yc.astype(BF16), yd_ref[0])
    merged = None
    for i in range(N_BRANCH):
        term = jax.nn.sigmoid(gates[:, i * D_MODEL:(i + 1) * D_MODEL]) * _dot(ys[i], wb_ref[i])
        merged = term if merged is None else merged + term
    hn = x + mod[2:3] * _dot(merged.astype(BF16), wo_ref[...])
    hout_ref[0] = hn
    f = _norm_modulate(hn, g2_ref[...], mod, 3, 4)
    f_ref[0] = f
    fh, fl = _split_bf16(f)
    lg_ref[0] = _dot(fh, rwh_ref[...]) + _dot(fl, rwh_ref[...]) + _dot(fh, rwl_ref[...]) + rb_ref[...]


def _merge(hall, g1, modall, ya, yb, ys, u, yd, s5p, gluw, wg, wb, wo, g2, rwh, rwl, rbias, nct):
    b, t, _ = hall.shape
    tl = ROW_TILE
    row = lambda w: pl.BlockSpec((1, tl, w), lambda b, t: (b, t, 0))
    return pl.pallas_call(
        _merge_kernel,
        out_shape=(
            jax.ShapeDtypeStruct((b, t, D_MODEL), F32),
            jax.ShapeDtypeStruct((b, t, D_MODEL), F32),
            jax.ShapeDtypeStruct((b, t, LOGIT_W), F32),
        ),
        grid=(b, t // tl),
        in_specs=[
            row(D_MODEL), _const_spec((1, D_MODEL)), _mod_spec(nct),
            row(BRANCH_W), row(BRANCH_W), row(BRANCH_W), row(BRANCH_W), row(BRANCH_W),
            _const_spec((8, BRANCH_W)), _const_spec((BRANCH_W, BRANCH_W)),
            _const_spec((D_MODEL, N_BRANCH * D_MODEL)), _const_spec((N_BRANCH, BRANCH_W, D_MODEL)),
            _const_spec((D_MODEL, D_MODEL)), _const_spec((1, D_MODEL)),
            _const_spec((D_MODEL, LOGIT_W)), _const_spec((D_MODEL, LOGIT_W)), _const_spec((1, LOGIT_W)),
        ],
        out_specs=(row(D_MODEL), row(D_MODEL), row(LOGIT_W)),
        compiler_params=pltpu.CompilerParams(dimension_semantics=("parallel", "parallel"),
                                             vmem_limit_bytes=VMEM_LIMIT),
        name="merge_router",
    )(hall, g1, modall, ya, yb, ys, u, yd, s5p, gluw, wg, wb, wo, g2, rwh, rwl, rbias)


def _route_kernel(lg_ref, idx_ref, gate_ref):
    x = lg_ref[...]
    lane = lax.broadcasted_iota(jnp.int32, x.shape, 1).astype(F32)
    big = 1e9
    ninf = -jnp.inf
    in_g = lane < MOE_GROUPS
    m1 = jnp.max(jnp.where(in_g, x, ninf), axis=-1, keepdims=True)
    grp = jnp.min(jnp.where(in_g & (x == m1), lane, big), axis=-1, keepdims=True)
    p_grp = 1.0 / jnp.sum(jnp.where(in_g, jnp.exp(x - m1), 0.0), axis=-1, keepdims=True)
    lo = MOE_GROUPS + grp * EXPERTS_PER_GROUP
    in_e = (lane >= lo) & (lane < lo + EXPERTS_PER_GROUP)
    v0 = jnp.max(jnp.where(in_e, x, ninf), axis=-1, keepdims=True)
    i0 = jnp.min(jnp.where(in_e & (x == v0), lane, big), axis=-1, keepdims=True)
    rest = in_e & (lane != i0)
    v1 = jnp.max(jnp.where(rest, x, ninf), axis=-1, keepdims=True)
    i1 = jnp.min(jnp.where(rest & (x == v1), lane, big), axis=-1, keepdims=True)
    e1 = jnp.exp(v1 - v0)
    g0 = p_grp / (1.0 + e1)
    g1 = p_grp * e1 / (1.0 + e1)
    idx = jnp.where(lane == 0.0, i0 - MOE_GROUPS, jnp.where(lane == 1.0, i1 - MOE_GROUPS, 0.0))
    idx_ref[...] = idx.astype(jnp.int32)
    gate_ref[...] = jnp.where(lane == 0.0, g0, jnp.where(lane == 1.0, g1, 0.0))


def _route(logits):
    n = logits.shape[0]
    tl = ROW_TILE
    spec = pl.BlockSpec((tl, LOGIT_W), lambda i: (i, 0))
    return pl.pallas_call(
        _route_kernel,
        out_shape=(jax.ShapeDtypeStruct((n, LOGIT_W), jnp.int32), jax.ShapeDtypeStruct((n, LOGIT_W), F32)),
        grid=(n // tl,),
        in_specs=[spec],
        out_specs=(spec, spec),
        compiler_params=pltpu.CompilerParams(dimension_semantics=("parallel",)),
        name="moe_route",
    )(logits)


def _dispatch_plan(expert):
    n = expert.shape[0]
    mb = MOE_ROWS
    e_flat = expert.reshape(-1)
    onehot = (e_flat[:, None] == jnp.arange(N_EXPERTS, dtype=jnp.int32)[None, :]).astype(jnp.int32)
    seg = 512
    oh3 = onehot.reshape(-1, seg, N_EXPERTS).astype(BF16)
    tril = jnp.asarray(np.tril(np.ones((seg, seg), np.float32)), BF16)
    within = jnp.einsum('st,ntc->nsc', tril, oh3, preferred_element_type=F32)
    seg_tot = within[:, -1, :]
    seg_base = jnp.cumsum(seg_tot, axis=0) - seg_tot
    csum = (within + seg_base[:, None, :]).astype(jnp.int32).reshape(-1, N_EXPERTS)
    counts = csum[-1]
    padded = (counts + mb - 1) // mb * mb
    pad_ends = jnp.cumsum(padded)
    pad_starts = pad_ends - padded
    dest = jnp.sum(onehot * (csum - 1 + pad_starts[None, :]), axis=1)
    n_blocks = -(-(2 * n + N_EXPERTS * (mb - 1)) // mb)
    tok = jnp.repeat(jnp.arange(n, dtype=jnp.int32), 2)
    buf_tok = jnp.zeros((n_blocks * mb,), jnp.int32).at[dest].set(tok)
    blk_expert = jnp.minimum(
        jnp.searchsorted(pad_ends, jnp.arange(n_blocks, dtype=jnp.int32) * mb, side='right'),
        N_EXPERTS - 1).astype(jnp.int32)
    return dest.astype(jnp.int32), buf_tok.reshape(n_blocks, 1, mb), blk_expert


def _row_gather_start(idx_ref, src_hbm, dst, sem, nrows):
    for r in range(nrows):
        pltpu.make_async_copy(src_hbm.at[pl.ds(idx_ref[0, 0, r], 1)], dst.at[pl.ds(r, 1)], sem).start(priority=r % 2)


def _row_gather_wait(src_hbm, dst, sem, nrows):
    pltpu.make_async_copy(src_hbm.at[pl.ds(0, nrows)], dst, sem).wait()


def _ffn_kernel(be_ref, tok_ref, tokn_ref, f_hbm, w1_ref, w3_ref, w2_ref, y_ref, xbuf, sem):
    i = pl.program_id(0)
    nblk = pl.num_programs(0)
    slot = lax.rem(i, 2)

    @pl.when(i == 0)
    def _():
        _row_gather_start(tok_ref, f_hbm, xbuf.at[0], sem.at[0], MOE_ROWS)

    _row_gather_start(tokn_ref, f_hbm, xbuf.at[1 - slot], sem.at[1 - slot], MOE_ROWS)
    _row_gather_wait(f_hbm, xbuf.at[slot], sem.at[slot], MOE_ROWS)
    x = xbuf[slot].astype(BF16)
    a = _dot(x, w1_ref[0])
    hid = a * jax.nn.sigmoid(a) * _dot(x, w3_ref[0])
    y_ref[...] = _dot(hid.astype(BF16), w2_ref[0])

    @pl.when(i == nblk - 1)
    def _():
        _row_gather_wait(f_hbm, xbuf.at[1 - slot], sem.at[1 - slot], MOE_ROWS)


def _expert_ffn(f, buf_tok, blk_expert, w1, w3, w2):
    n_blocks = buf_tok.shape[0]
    mb = MOE_ROWS
    idx_spec = lambda off: pl.BlockSpec((1, 1, mb), lambda i, be: (jnp.minimum(i + off, n_blocks - 1), 0, 0),
                                        memory_space=pltpu.SMEM)
    return pl.pallas_call(
        _ffn_kernel,
        out_shape=jax.ShapeDtypeStruct((n_blocks * mb, D_MODEL), F32),
        grid_spec=pltpu.PrefetchScalarGridSpec(
            num_scalar_prefetch=1,
            grid=(n_blocks,),
            in_specs=[
                idx_spec(0), idx_spec(1),
                pl.BlockSpec(memory_space=pl.ANY),
                pl.BlockSpec((1, D_MODEL, D_FF_EXPERT), lambda i, be: (be[i], 0, 0)),
                pl.BlockSpec((1, D_MODEL, D_FF_EXPERT), lambda i, be: (be[i], 0, 0)),
                pl.BlockSpec((1, D_FF_EXPERT, D_MODEL), lambda i, be: (be[i], 0, 0)),
            ],
            out_specs=pl.BlockSpec((mb, D_MODEL), lambda i, be: (i, 0)),
            scratch_shapes=[pltpu.VMEM((2, mb, D_MODEL), F32), pltpu.SemaphoreType.DMA((2,))],
        ),
        compiler_params=pltpu.CompilerParams(dimension_semantics=("arbitrary",), vmem_limit_bytes=VMEM_LIMIT),
        name="moe_expert_ffn",
    )(blk_expert, buf_tok, buf_tok, f, w1, w3, w2)


def _combine_kernel(d_ref, dn_ref, h_ref, mod_ref, gate_ref, y_hbm, o_ref, rbuf, sem):
    i = pl.program_id(0)
    nsteps = pl.num_programs(0)
    slot = lax.rem(i, 2)
    tl = ROW_TILE

    @pl.when(i == 0)
    def _():
        _row_gather_start(d_ref, y_hbm, rbuf.at[0], sem.at[0], 2 * tl)

    _row_gather_start(dn_ref, y_hbm, rbuf.at[1 - slot], sem.at[1 - slot], 2 * tl)
    _row_gather_wait(y_hbm, rbuf.at[slot], sem.at[slot], 2 * tl)
    g = gate_ref[...]
    out = g[:, 0:1] * rbuf[slot, 0:tl, :] + g[:, 1:2] * rbuf[slot, tl:2 * tl, :]
    o_ref[...] = h_ref[...] + mod_ref[0, 0][5:6] * out

    @pl.when(i == nsteps - 1)
    def _():
        _row_gather_wait(y_hbm, rbuf.at[1 - slot], sem.at[1 - slot], 2 * tl)


def _combine(hflat, modall, gate, dest, yb, tiles_per_sample, nct):
    n = hflat.shape[0]
    tl = ROW_TILE
    nsteps = n // tl
    d2 = dest.reshape(nsteps, tl, 2).transpose(0, 2, 1).reshape(nsteps, 1, 2 * tl)
    idx_spec = lambda off: pl.BlockSpec((1, 1, 2 * tl), lambda i: (jnp.minimum(i + off, nsteps - 1), 0, 0),
                                        memory_space=pltpu.SMEM)
    row = lambda w: pl.BlockSpec((tl, w), lambda i: (i, 0))

    def mod_map(i):
        return (i // tiles_per_sample, jnp.where(i % tiles_per_sample >= nct, 1, 0), 0, 0)

    return pl.pallas_call(
        _combine_kernel,
        out_shape=jax.ShapeDtypeStruct((n, D_MODEL), F32),
        grid=(nsteps,),
        in_specs=[
            idx_spec(0), idx_spec(1), row(D_MODEL),
            pl.BlockSpec((1, 1, 8, D_MODEL), mod_map),
            row(LOGIT_W), pl.BlockSpec(memory_space=pl.ANY),
        ],
        out_specs=row(D_MODEL),
        scratch_shapes=[pltpu.VMEM((2, 2 * tl, D_MODEL), F32), pltpu.SemaphoreType.DMA((2,))],
        compiler_params=pltpu.CompilerParams(dimension_semantics=("arbitrary",), vmem_limit_bytes=VMEM_LIMIT),
        name="moe_combine",
    )(d2, d2, hflat, modall, gate, yb)


def _rope_tables(lat_len, ctx_len):
    pos = jnp.arange(lat_len, dtype=jnp.int32)
    n_freq = HEAD_DIM // 4
    inv = ROPE_BASE ** (-jnp.arange(n_freq, dtype=F32) / n_freq)
    ang_r = (pos // GRID_W).astype(F32)[:, None] * inv
    ang_c = (pos % GRID_W).astype(F32)[:, None] * inv
    zero = jnp.zeros_like(ang_r)
    cos = jnp.concatenate([jnp.cos(ang_r)] * 2 + [jnp.cos(ang_c)] * 2, axis=-1)
    sin_up = jnp.concatenate([-jnp.sin(ang_r), zero, -jnp.sin(ang_c), zero], axis=-1)
    sin_dn = jnp.concatenate([zero, jnp.sin(ang_r), zero, jnp.sin(ang_c)], axis=-1)

    def full(tab, ctx_val):
        tab = jnp.tile(tab, (1, N_HEADS))
        return jnp.concatenate([jnp.full((ctx_len, BRANCH_W), ctx_val, F32), tab], axis=0)

    return full(cos, 1.0), full(sin_up, 0.0), full(sin_dn, 0.0)


def kernel(x, c, ctx, c_ctx, mod_w, mod_b, norm1_g, norm2_g, w_in, w_branch, w_out, ret_log_decay,
           na_qk_gain, na_rpb, s5_lambda_re, s5_lambda_im, s5_log_step, s5_b_re, s5_b_im, s5_c_re,
           s5_c_im, s5_d, s5_glu_w, s5_glu_b, gqa_qk_gain, gqa_sink, router_w1, router_b1, router_w2,
           router_b2, exp_w1, exp_w3, exp_w2):
    depth = mod_w.shape[0]
    bsz, lat_len, _ = x.shape
    ctx_len = ctx.shape[1]
    t = ctx_len + lat_len
    nct = ctx_len // ROW_TILE
    assert ctx_len % ROW_TILE == 0 and lat_len % (NA_QROWS * GRID_W) == 0 and ctx_len % RET_CHUNK == 0
    assert lat_len // GRID_W >= NA_KROWS and lat_len >= GQA_KT

    hall = jnp.concatenate([ctx, x], axis=1)
    modall = _modulation(c, c_ctx, mod_w, mod_b)
    rope_tabs = _rope_tables(lat_len, ctx_len)
    ones_bd = jnp.asarray(np.kron(np.eye(N_HEADS), np.full((HEAD_DIM, HEAD_DIM), 1.0 / HEAD_DIM)), BF16)

    for i in range(depth):
        g1 = norm1_g[i].reshape(1, D_MODEL)
        g2 = norm2_g[i].reshape(1, D_MODEL)
        w_mix = w_in[i, :, :MIX_COLS].astype(BF16)
        w_gate = w_in[i, :, MIX_COLS:].astype(BF16)
        gains = jnp.concatenate([jnp.tile(na_qk_gain[i], (1, N_HEADS)), jnp.tile(gqa_qk_gain[i], (1, N_HEADS))], 0)
        ret, na, u, gqa = _inproj(hall, g1, modall[i], w_mix, rope_tabs, gains, ones_bd, nct)

        dmask, rtabs, cdec = _ret_consts(ret_log_decay[i])
        ya = _retention(ret, dmask, rtabs, cdec, ctx_len // RET_CHUNK)
        yb = _neighborhood(na, _na_bias_tables(na_rpb[i], lat_len // GRID_W), ctx_len)
        s5w, s5cp, s5scan = _s5_consts(s5_lambda_re[i], s5_lambda_im[i], s5_log_step[i], s5_b_re[i], s5_b_im[i],
                                       s5_c_re[i], s5_c_im[i])
        ys = _s5_core(u, s5w, s5cp, s5scan, ctx_len)
        yd = _window_gqa(gqa, gqa_sink[i], ctx_len)

        s5p = jnp.zeros((8, BRANCH_W), F32).at[0].set(s5_d[i]).at[1].set(s5_glu_b[i])
        rw = jnp.zeros((D_MODEL, LOGIT_W), F32).at[:, :MOE_GROUPS].set(router_w1[i])
        rw = rw.at[:, MOE_GROUPS:MOE_GROUPS + N_EXPERTS].set(router_w2[i])
        rwh, rwl = _split_bf16(rw)
        rbias = jnp.zeros((1, LOGIT_W), F32).at[0, :MOE_GROUPS].set(router_b1[i])
        rbias = rbias.at[0, MOE_GROUPS:MOE_GROUPS + N_EXPERTS].set(router_b2[i])
        hall, f, logits = _merge(hall, g1, modall[i], ya, yb, ys, u, yd, s5p, s5_glu_w[i].astype(BF16), w_gate,
                                 w_branch[i].astype(BF16), w_out[i].astype(BF16), g2, rwh, rwl, rbias, nct)

        n = bsz * t
        idx, gate = _route(logits.reshape(n, LOGIT_W))
        dest, buf_tok, blk_expert = _dispatch_plan(idx[:, :2])
        yexp = _expert_ffn(f.reshape(n, D_MODEL), buf_tok, blk_expert, exp_w1[i].astype(BF16),
                           exp_w3[i].astype(BF16), exp_w2[i].astype(BF16))
        hall = _combine(hall.reshape(n, D_MODEL), modall[i], gate, dest, yexp, t // ROW_TILE, nct)
        hall = hall.reshape(bsz, t, D_MODEL)

    return hall[:, ctx_len:]
```

```python
import functools
import math

import numpy as np
import jax
import jax.numpy as jnp
from jax import lax
from jax.experimental import pallas as pl
from jax.experimental.pallas import tpu as pltpu

F32 = jnp.float32
BF16 = jnp.bfloat16

D_MODEL = 1024
GRID_W = 64
HEAD_DIM = 64
N_BRANCH = 4
BRANCH_W = D_MODEL // 4
N_HEADS = BRANCH_W // HEAD_DIM
RET_CHUNK = 128
NA_KH = 8
NA_KW = 16
NA_QROWS = 4
NA_KROWS = NA_QROWS + NA_KH
S5_GROUP_CH = 16
S5_GROUPS = BRANCH_W // S5_GROUP_CH
S5_STATE = 64
S5_CHUNK = 16
GQA_KV_HEADS = 2
GQA_KV_W = GQA_KV_HEADS * HEAD_DIM
WIN = 128
MOE_GROUPS = 4
EXPERTS_PER_GROUP = 8
N_EXPERTS = MOE_GROUPS * EXPERTS_PER_GROUP
D_FF_EXPERT = D_MODEL // 2
ROPE_BASE = 10000.0
EPS = 1e-6
NEG = -1e30
MIX_COLS = 9 * BRANCH_W + 2 * GQA_KV_W

ROW_TILE = 256
MOE_ROWS = 256
LOGIT_W = 128
VMEM_LIMIT = 56 * 1024 * 1024


def _const_spec(shape):
    nd = len(shape)
    return pl.BlockSpec(shape, lambda *_: (0,) * nd)


def _dot(a, b):
    return jnp.dot(a, b, preferred_element_type=F32)


def _dot_nt(a, b):
    return lax.dot_general(a, b, (((1,), (1,)), ((), ())), preferred_element_type=F32)


def _dot_tn(a, b):
    return lax.dot_general(a, b, (((0,), (0,)), ((), ())), preferred_element_type=F32)


def _split_bf16(x):
    hi = x.astype(BF16)
    lo = (x - hi.astype(F32)).astype(BF16)
    return hi, lo


def _norm_modulate(x, gain, mod, shift_row, scale_row):
    ms = jnp.mean(x * x, axis=-1, keepdims=True)
    a = x * lax.rsqrt(ms + EPS) * gain
    return a * (1.0 + mod[scale_row:scale_row + 1]) + mod[shift_row:shift_row + 1]


def _mod_kernel(c_ref, w_ref, b_ref, o_ref):
    c = c_ref[...]
    s = c * jax.nn.sigmoid(c)
    o_ref[0] = _dot(s.astype(BF16), w_ref[0].astype(BF16)) + b_ref[0]


def _modulation(c, c_ctx, mod_w, mod_b):
    depth = mod_w.shape[0]
    b = c.shape[0]
    rows = -(-(b + 1) // 8) * 8
    cs = jnp.zeros((rows, D_MODEL), F32).at[:b].set(c).at[b].set(c_ctx)
    out = pl.pallas_call(
        _mod_kernel,
        out_shape=jax.ShapeDtypeStruct((depth, rows, 6 * D_MODEL), F32),
        grid=(depth, 6),
        in_specs=[
            _const_spec((rows, D_MODEL)),
            pl.BlockSpec((1, D_MODEL, D_MODEL), lambda l, j: (l, 0, j)),
            pl.BlockSpec((1, 1, D_MODEL), lambda l, j: (l, 0, j)),
        ],
        out_specs=pl.BlockSpec((1, rows, D_MODEL), lambda l, j: (l, 0, j)),
        compiler_params=pltpu.CompilerParams(dimension_semantics=("parallel", "parallel")),
        name="adaln_mod",
    )(cs, mod_w, mod_b.reshape(depth, 1, 6 * D_MODEL))
    lat = out[:, :b].reshape(depth, b, 6, D_MODEL)
    ctx = jnp.broadcast_to(out[:, b].reshape(depth, 1, 6, D_MODEL), (depth, b, 6, D_MODEL))
    modall = jnp.stack([ctx, lat], axis=2)
    return jnp.pad(modall, ((0, 0), (0, 0), (0, 0), (0, 2), (0, 0)))


def _head_mean_sq(x, ones_bd):
    hi, lo = _split_bf16(x * x)
    return _dot(hi, ones_bd) + _dot(lo, ones_bd)


def _rope(x, cos, sin_up, sin_dn):
    w = x.shape[-1]
    return x * cos + pltpu.roll(x, w - 16, 1) * sin_up + pltpu.roll(x, 16, 1) * sin_dn


def _inproj_kernel(h_ref, g1_ref, mod_ref, w_ref, cos_ref, sup_ref, sdn_ref, gains_ref, ones_ref,
                   ret_ref, na_ref, u_ref, gqa_ref, ug_ref):
    a = _norm_modulate(h_ref[0], g1_ref[...], mod_ref[0, 0], 0, 1)
    p = _dot(a.astype(BF16), w_ref[...])
    cos, sup, sdn = cos_ref[...], sup_ref[...], sdn_ref[...]
    ones_bd = ones_ref[...]
    scale = HEAD_DIM ** -0.5
    bw = BRANCH_W

    def col(i):
        return p[:, i * bw:(i + 1) * bw]

    def head_rms(x, gain):
        w = x.shape[-1]
        ms = _head_mean_sq(x, ones_bd[:w, :w])
        return x * lax.rsqrt(ms + EPS) * gain

    ret_ref[0, :, 0 * bw:1 * bw] = (_rope(col(0), cos, sup, sdn) * scale).astype(BF16)
    ret_ref[0, :, 1 * bw:2 * bw] = _rope(col(1), cos, sup, sdn).astype(BF16)
    ret_ref[0, :, 2 * bw:3 * bw] = col(2).astype(BF16)
    ret_ref[0, :, 3 * bw:4 * bw] = col(3).astype(BF16)
    na_ref[0, :, 0 * bw:1 * bw] = (head_rms(col(4), gains_ref[0:1]) * scale).astype(BF16)
    na_ref[0, :, 1 * bw:2 * bw] = head_rms(col(5), gains_ref[1:2]).astype(BF16)
    na_ref[0, :, 2 * bw:3 * bw] = col(6).astype(BF16)
    u_ref[0] = col(7)
    ug_ref[...] = pltpu.einshape("(nt)(gc)->gn(tc)", col(7), t=S5_CHUNK, g=S5_GROUPS).astype(BF16)
    gq = _rope(head_rms(col(8), gains_ref[2:3]), cos, sup, sdn) * scale
    gqa_ref[0, :, 0:bw] = gq.astype(BF16)
    kw = GQA_KV_W
    gk = head_rms(p[:, 9 * bw:9 * bw + kw], gains_ref[3:4, :kw])
    gk = _rope(gk, cos[:, :kw], sup[:, :kw], sdn[:, :kw])
    gqa_ref[0, :, bw:bw + kw] = gk.astype(BF16)
    gqa_ref[0, :, bw + kw:bw + 2 * kw] = p[:, 9 * bw + kw:9 * bw + 2 * kw].astype(BF16)


def _mod_spec(nct):
    return pl.BlockSpec((1, 1, 8, D_MODEL), lambda b, t: (b, jnp.where(t >= nct, 1, 0), 0, 0))


def _inproj(hall, g1, modall, w_mix, rope_tabs, gains, ones_bd, nct):
    b, t, _ = hall.shape
    tl = ROW_TILE
    row = lambda w: pl.BlockSpec((1, tl, w), lambda b, t: (b, t, 0))
    tab = pl.BlockSpec((tl, BRANCH_W), lambda b, t: (t, 0))
    return pl.pallas_call(
        _inproj_kernel,
        out_shape=(
            jax.ShapeDtypeStruct((b, t, 4 * BRANCH_W), BF16),
            jax.ShapeDtypeStruct((b, t, 3 * BRANCH_W), BF16),
            jax.ShapeDtypeStruct((b, t, BRANCH_W), F32),
            jax.ShapeDtypeStruct((b, t, BRANCH_W + 2 * GQA_KV_W), BF16),
            jax.ShapeDtypeStruct((S5_GROUPS, t // S5_CHUNK, b * BRANCH_W), BF16),
        ),
        grid=(b, t // tl),
        in_specs=[
            row(D_MODEL), _const_spec((1, D_MODEL)), _mod_spec(nct), _const_spec((D_MODEL, MIX_COLS)),
            tab, tab, tab, _const_spec((4, BRANCH_W)), _const_spec((BRANCH_W, BRANCH_W)),
        ],
        out_specs=(row(4 * BRANCH_W), row(3 * BRANCH_W), row(BRANCH_W), row(BRANCH_W + 2 * GQA_KV_W),
                   pl.BlockSpec((S5_GROUPS, tl // S5_CHUNK, BRANCH_W), lambda b, t: (0, t, b))),
        compiler_params=pltpu.CompilerParams(dimension_semantics=("parallel", "parallel"),
                                             vmem_limit_bytes=VMEM_LIMIT),
        name="inproj",
    )(hall, g1, modall, w_mix, *rope_tabs, gains, ones_bd)


def _ret_consts(log_decay):
    lg = -jnp.exp(log_decay.astype(F32))
    idx = jnp.arange(RET_CHUNK, dtype=F32)
    diff = idx[:, None] - idx[None, :]
    fwd = jnp.where(diff > 0, jnp.exp(jnp.maximum(diff, 0.0)[None] * lg[0][:, None, None]), 0.0)
    bwd = jnp.where(diff < 0, jnp.exp(jnp.maximum(-diff, 0.0)[None] * lg[1][:, None, None]), 0.0)
    dmask = fwd + bwd + 2.0 * jnp.eye(RET_CHUNK, dtype=F32)[None]
    tabs = jnp.stack([
        jnp.exp((RET_CHUNK - 1 - idx)[None, :] * lg[0][:, None]),
        jnp.exp(idx[None, :] * lg[1][:, None]),
        jnp.exp((idx + 1.0)[None, :] * lg[0][:, None]),
        jnp.exp((RET_CHUNK - idx)[None, :] * lg[1][:, None]),
    ])
    tabs = jnp.repeat(tabs.transpose(0, 2, 1), HEAD_DIM, axis=-1)
    cdec = jnp.exp(RET_CHUNK * lg).reshape(2 * N_HEADS)
    return dmask, tabs, cdec


def _ret_bwd_chunk(i, ncc, nch):
    return jnp.where(i < ncc, ncc - 1 - i, nch - 1 - (i - ncc))


def _ret_kernel(cdec_ref, q_ref, k_ref, v_ref, g_ref, dmask_ref, tab_ref, o_ref, sf, sb, sbprev, *, ncc):
    ph = pl.program_id(1)
    i = pl.program_id(2)
    nch = pl.num_programs(2)
    hd = HEAD_DIM

    @pl.when(jnp.logical_and(ph == 0, i == 0))
    def _():
        sb[...] = jnp.zeros_like(sb)

    @pl.when(jnp.logical_and(ph == 1, i == 0))
    def _():
        sf[...] = jnp.zeros_like(sf)

    k = k_ref[0]
    v = v_ref[0]

    @pl.when(ph == 0)
    def _():
        n = _ret_bwd_chunk(i, ncc, nch)
        kd = (k.astype(F32) * tab_ref[1]).astype(BF16)
        kv = _dot_tn(kd, v)
        for h in range(N_HEADS):
            sbprev[n, h] = sb[h]
            sb[h] = cdec_ref[N_HEADS + h] * sb[h] + kv[h * hd:(h + 1) * hd, h * hd:(h + 1) * hd]

    @pl.when(ph == 1)
    def _():
        q = q_ref[0]
        qf32 = q.astype(F32)
        qf = (qf32 * tab_ref[2]).astype(BF16)
        qb = (qf32 * tab_ref[3]).astype(BF16)
        kd = (k.astype(F32) * tab_ref[0]).astype(BF16)
        kv = _dot_tn(kd, v)
        outs = []
        for h in range(N_HEADS):
            hs = slice(h * hd, (h + 1) * hd)
            s = _dot_nt(q[:, hs], k[:, hs]) * dmask_ref[h]
            o = _dot(s.astype(BF16), v[:, hs])
            o = o + _dot(qf[:, hs], sf[h].astype(BF16)) + _dot(qb[:, hs], sbprev[i, h].astype(BF16))
            sf[h] = cdec_ref[h] * sf[h] + kv[hs, hs]
            outs.append(o * lax.rsqrt(jnp.mean(o * o, axis=-1, keepdims=True) + EPS))
        g = g_ref[0].astype(F32)
        o_ref[0] = (jnp.concatenate(outs, axis=-1) * (g * jax.nn.sigmoid(g))).astype(BF16)


def _retention(ret, dmask, tabs, cdec, ncc):
    b, t, _ = ret.shape
    nch = t // RET_CHUNK
    bw = BRANCH_W

    def kv_map(col):
        return lambda b, ph, i, *_: (b, jnp.where(ph == 0, _ret_bwd_chunk(i, ncc, nch), i), col)

    def q_map(col):
        return lambda b, ph, i, *_: (b, jnp.where(ph == 0, 0, i), col)

    blk = (1, RET_CHUNK, bw)
    return pl.pallas_call(
        functools.partial(_ret_kernel, ncc=ncc),
        out_shape=jax.ShapeDtypeStruct((b, t, bw), BF16),
        grid_spec=pltpu.PrefetchScalarGridSpec(
            num_scalar_prefetch=1,
            grid=(b, 2, nch),
            in_specs=[
                pl.BlockSpec(blk, q_map(0)), pl.BlockSpec(blk, kv_map(1)), pl.BlockSpec(blk, kv_map(2)),
                pl.BlockSpec(blk, q_map(3)),
                _const_spec((N_HEADS, RET_CHUNK, RET_CHUNK)), _const_spec((4, RET_CHUNK, bw)),
            ],
            out_specs=pl.BlockSpec(blk, q_map(0)),
            scratch_shapes=[
                pltpu.VMEM((N_HEADS, HEAD_DIM, HEAD_DIM), F32),
                pltpu.VMEM((N_HEADS, HEAD_DIM, HEAD_DIM), F32),
                pltpu.VMEM((nch, N_HEADS, HEAD_DIM, HEAD_DIM), F32),
            ],
        ),
        compiler_params=pltpu.CompilerParams(dimension_semantics=("parallel", "arbitrary", "arbitrary"),
                                             vmem_limit_bytes=VMEM_LIMIT),
        name="retention",
    )(cdec, ret, ret, ret, ret, dmask, tabs)


def _na_bias_tables(rpb, rows):
    nrb = rows // NA_QROWS
    qi = np.arange(NA_QROWS)
    kl = np.arange(NA_KROWS)
    ridx, rvalid = [], []
    for rb in (0, 1, nrb - 1):
        u0 = int(np.clip(NA_QROWS * rb - NA_KH // 2, 0, rows - NA_KROWS))
        r = NA_QROWS * rb + qi[:, None]
        kr = u0 + kl[None, :]
        r0 = np.clip(r - NA_KH // 2, 0, rows - NA_KH)
        rvalid.append((kr >= r0) & (kr < r0 + NA_KH))
        ridx.append(np.clip(kr - r + NA_KH - 1, 0, 2 * NA_KH - 2))
    ridx, rvalid = np.stack(ridx), np.stack(rvalid)
    qc = np.arange(GRID_W)[:, None]
    kc = np.arange(GRID_W)[None, :]
    ws = np.clip(qc - NA_KW // 2, 0, GRID_W - NA_KW)
    cvalid = (kc >= ws) & (kc < ws + NA_KW)
    cidx = np.clip(kc - qc + NA_KW - 1, 0, 2 * NA_KW - 2)
    row_sel = jnp.asarray(ridx[..., None] == np.arange(2 * NA_KH - 1), F32)
    col_sel = jnp.asarray(cidx[..., None] == np.arange(2 * NA_KW - 1), F32)
    hp = lax.Precision.HIGHEST
    rows_b = jnp.einsum('tikr,hrc->thikc', row_sel, rpb.astype(F32), precision=hp)
    bias = jnp.einsum('thikc,qzc->thiqkz', rows_b, col_sel, precision=hp)
    valid = rvalid[:, None, :, None, :, None] & cvalid[None, None, None, :, None, :]
    bias = jnp.where(jnp.asarray(valid), bias, NEG)
    return bias.reshape(3, N_HEADS, NA_QROWS * GRID_W, NA_KROWS * GRID_W)


def _softmax_pv(parts, sink=None):
    m = functools.reduce(jnp.maximum, [jnp.max(s, axis=-1, keepdims=True) for s, _ in parts])
    if sink is not None:
        m = jnp.maximum(m, sink)
    den = None
    acc = None
    for s, v in parts:
        p = jnp.exp(s - m)
        d = jnp.sum(p, axis=-1, keepdims=True)
        a = _dot(p.astype(BF16), v)
        den = d if den is None else den + d
        acc = a if acc is None else acc + a
    if sink is not None:
        den = den + jnp.exp(sink - m)
    return acc / den


def _na_kernel(q_ref, k_ref, v_ref, bias_ref, o_ref, *, nct, ctx_len, rows):
    j = pl.program_id(1)
    hd = HEAD_DIM
    q = q_ref[0]
    kc = k_ref[0, 0:ctx_len, :]
    vc = v_ref[0, 0:ctx_len, :]

    @pl.when(j < nct)
    def _():
        outs = []
        for h in range(N_HEADS):
            hs = slice(h * hd, (h + 1) * hd)
            outs.append(_softmax_pv([(_dot_nt(q[:, hs], kc[:, hs]), vc[:, hs])]))
        o_ref[0] = jnp.concatenate(outs, axis=-1).astype(BF16)

    @pl.when(j >= nct)
    def _():
        rb = j - nct
        u0 = jnp.clip(NA_QROWS * rb - NA_KH // 2, 0, rows - NA_KROWS)
        start = pl.multiple_of(ctx_len + u0 * GRID_W, GRID_W)
        kw = k_ref[0, pl.ds(start, NA_KROWS * GRID_W), :]
        vw = v_ref[0, pl.ds(start, NA_KROWS * GRID_W), :]
        outs = []
        for h in range(N_HEADS):
            hs = slice(h * hd, (h + 1) * hd)
            s_nb = _dot_nt(q[:, hs], kw[:, hs]) + bias_ref[0, h]
            s_cx = _dot_nt(q[:, hs], kc[:, hs])
            outs.append(_softmax_pv([(s_nb, vw[:, hs]), (s_cx, vc[:, hs])]))
        o_ref[0] = jnp.concatenate(outs, axis=-1).astype(BF16)


def _neighborhood(na, bias, ctx_len):
    b, t, _ = na.shape
    bw = BRANCH_W
    qt = NA_QROWS * GRID_W
    nct = ctx_len // qt
    nsteps = t // qt
    nrb = nsteps - nct
    rows = nrb * NA_QROWS

    def bias_map(b, j):
        rb = j - nct
        return (jnp.where(rb <= 0, 0, jnp.where(rb == nrb - 1, 2, 1)), 0, 0, 0)

    return pl.pallas_call(
        functools.partial(_na_kernel, nct=nct, ctx_len=ctx_len, rows=rows),
        out_shape=jax.ShapeDtypeStruct((b, t, bw), BF16),
        grid=(b, nsteps),
        in_specs=[
            pl.BlockSpec((1, qt, bw), lambda b, j: (b, j, 0)),
            pl.BlockSpec((1, t, bw), lambda b, j: (b, 0, 1)),
            pl.BlockSpec((1, t, bw), lambda b, j: (b, 0, 2)),
            pl.BlockSpec((1, N_HEADS, qt, NA_KROWS * GRID_W), bias_map),
        ],
        out_specs=pl.BlockSpec((1, qt, bw), lambda b, j: (b, j, 0)),
        compiler_params=pltpu.CompilerParams(dimension_semantics=("parallel", "arbitrary"),
                                             vmem_limit_bytes=VMEM_LIMIT),
        name="neighborhood_attn",
    )(na, na, na, bias)


GQA_QT = 256
GQA_KT = GQA_QT + 2 * WIN


def _gqa_kernel(sink_ref, q_ref, kv_ref, o_ref, *, nct, ctx_len, lat_len):
    j = pl.program_id(1)
    hd = HEAD_DIM
    grp = N_HEADS // GQA_KV_HEADS
    q = q_ref[0]
    kvc = kv_ref[0, 0:ctx_len, :]

    def stacked_q(kvh):
        return jnp.concatenate([q[:, (kvh * grp + g) * hd:(kvh * grp + g + 1) * hd] for g in range(grp)], axis=0)

    def sink_col(kvh):
        return jnp.concatenate([jnp.full((GQA_QT, 1), sink_ref[kvh * grp + g], F32) for g in range(grp)], axis=0)

    def emit(results):
        cols = [results[kvh][g * GQA_QT:(g + 1) * GQA_QT] for kvh in range(GQA_KV_HEADS) for g in range(grp)]
        o_ref[0] = jnp.concatenate(cols, axis=-1).astype(BF16)

    @pl.when(j < nct)
    def _():
        res = []
        for kvh in range(GQA_KV_HEADS):
            kc = kvc[:, kvh * hd:(kvh + 1) * hd]
            vc = kvc[:, GQA_KV_W + kvh * hd:GQA_KV_W + (kvh + 1) * hd]
            res.append(_softmax_pv([(_dot_nt(stacked_q(kvh), kc), vc)], sink=sink_col(kvh)))
        emit(res)

    @pl.when(j >= nct)
    def _():
        q0 = (j - nct) * GQA_QT
        k0 = jnp.clip(q0 - WIN, 0, lat_len - GQA_KT)
        kvw = kv_ref[0, pl.ds(pl.multiple_of(ctx_len + k0, WIN), GQA_KT), :]
        qpos = q0 + (lax.broadcasted_iota(jnp.int32, (grp * GQA_QT, GQA_KT), 0) & (GQA_QT - 1))
        kpos = k0 + lax.broadcasted_iota(jnp.int32, (grp * GQA_QT, GQA_KT), 1)
        band = jnp.abs(qpos - kpos) <= WIN
        res = []
        for kvh in range(GQA_KV_HEADS):
            ks = slice(kvh * hd, (kvh + 1) * hd)
            vs = slice(GQA_KV_W + kvh * hd, GQA_KV_W + (kvh + 1) * hd)
            q2 = stacked_q(kvh)
            s_w = jnp.where(band, _dot_nt(q2, kvw[:, ks]), NEG)
            s_c = _dot_nt(q2, kvc[:, ks])
            res.append(_softmax_pv([(s_w, kvw[:, vs]), (s_c, kvc[:, vs])], sink=sink_col(kvh)))
        emit(res)


def _window_gqa(gqa, sink, ctx_len):
    b, t, _ = gqa.shape
    bw = BRANCH_W
    nct = ctx_len // GQA_QT
    return pl.pallas_call(
        functools.partial(_gqa_kernel, nct=nct, ctx_len=ctx_len, lat_len=t - ctx_len),
        out_shape=jax.ShapeDtypeStruct((b, t, bw), BF16),
        grid_spec=pltpu.PrefetchScalarGridSpec(
            num_scalar_prefetch=1,
            grid=(b, t // GQA_QT),
            in_specs=[
                pl.BlockSpec((1, GQA_QT, bw), lambda b, j, *_: (b, j, 0)),
                pl.BlockSpec((1, t, bw), lambda b, j, *_: (b, 0, 1)),
            ],
            out_specs=pl.BlockSpec((1, GQA_QT, bw), lambda b, j, *_: (b, j, 0)),
        ),
        compiler_params=pltpu.CompilerParams(dimension_semantics=("parallel", "arbitrary"),
                                             vmem_limit_bytes=VMEM_LIMIT),
        name="window_gqa",
    )(sink.astype(F32), gqa, gqa)


def _cmul(ar, ai, br, bi):
    return ar * br - ai * bi, ar * bi + ai * br


def _s5_consts(lam_re, lam_im, log_step, b_re, b_im, c_re, c_im):
    f32 = F32
    tc = S5_CHUNK
    cre, cim = c_re.astype(f32), c_im.astype(f32)
    bre, bim = b_re.astype(f32), b_im.astype(f32)
    kern, inj, cpow, scan = [], [], [], []
    for d in range(2):
        lre, lim, dt = lam_re[d].astype(f32), lam_im[d].astype(f32), jnp.exp(log_step[d].astype(f32))[:, None]
        mag = jnp.exp(lre * dt)
        ar, ai = mag * jnp.cos(lim * dt), mag * jnp.sin(lim * dt)
        den = lre * lre + lim * lim
        nr = ar - 1.0
        coef_re = (nr * lre + ai * lim) / den
        coef_im = (ai * lre - nr * lim) / den
        bbr = coef_re[..., None] * bre - coef_im[..., None] * bim
        bbi = coef_re[..., None] * bim + coef_im[..., None] * bre
        pr, pi = [jnp.ones_like(ar)], [jnp.zeros_like(ar)]
        for _ in range(tc):
            nr_, ni_ = _cmul(pr[-1], pi[-1], ar, ai)
            pr.append(nr_)
            pi.append(ni_)
        pr, pi = jnp.stack(pr), jnp.stack(pi)
        lbr, lbi = _cmul(pr[:tc, :, :, None], pi[:tc, :, :, None], bbr[None], bbi[None])
        kern.append(jnp.einsum('gcp,tgpd->tgcd', cre, lbr, precision=lax.Precision.HIGHEST)
                    - jnp.einsum('gcp,tgpd->tgcd', cim, lbi, precision=lax.Precision.HIGHEST))
        inj.append((lbr[::-1], lbi[::-1]) if d == 0 else (lbr, lbi))
        pwr, pwi = (pr[1:], pi[1:]) if d == 0 else (pr[1:][::-1], pi[1:][::-1])
        dr, di = _cmul(cre[None], cim[None], pwr[:, :, None, :], pwi[:, :, None, :])
        cpow.append((dr, -di))
        scan.append((pr[tc], pi[tc]))
    g, gc, p = S5_GROUPS, S5_GROUP_CH, S5_STATE
    lag = np.arange(tc)[None, :] - np.arange(tc)[:, None]
    taus = np.arange(tc)[:, None, None]
    sel_f = jnp.asarray(lag[None] == taus, F32)
    sel_b = jnp.asarray(-lag[None] == taus, F32)
    hp = lax.Precision.HIGHEST
    toep = (jnp.einsum('xst,xgcd->gsdtc', sel_f, kern[0], precision=hp)
            + jnp.einsum('xst,xgcd->gsdtc', sel_b, kern[1], precision=hp))
    toep = toep.reshape(g, tc * gc, tc * gc)
    injm = jnp.concatenate([x.transpose(1, 0, 3, 2).reshape(g, tc * gc, p)
                            for x in (inj[0][0], inj[0][1], inj[1][0], inj[1][1])], axis=-1)
    w = jnp.concatenate([toep, injm], axis=-1)
    cp = jnp.concatenate([x.transpose(1, 3, 0, 2).reshape(g, p, tc * gc)
                          for x in (cpow[0][0], cpow[0][1], cpow[1][0], cpow[1][1])], axis=1)
    rows = []
    for d in range(2):
        sr, si = scan[d]
        rows += [jnp.concatenate([sr, sr], -1), jnp.concatenate([-si, si], -1)]
    scan_c = jnp.stack(rows, axis=1)
    scan_c = jnp.pad(scan_c, ((0, 0), (0, 4), (0, 0)))
    return w.astype(BF16), cp.astype(BF16), scan_c


def _s5_kernel(u_ref, w_ref, cp_ref, sc_ref, y_ref, inj, injs, xs, *, bsz, nch, ncc):
    p2 = 2 * S5_STATE
    r = _dot(u_ref[0], w_ref[0])
    y_ref[0] = r[:, :BRANCH_W]
    rf = r[:, BRANCH_W:BRANCH_W + p2]
    rb = r[:, BRANCH_W + p2:]
    inj[:, 0:p2] = rf
    inj[:, p2:] = rb
    injs[:, 0:p2] = pltpu.roll(rf, S5_STATE, 1)
    injs[:, p2:] = pltpu.roll(rb, S5_STATE, 1)
    a_f, s_f, a_b, s_b = sc_ref[0, 0:1], sc_ref[0, 1:2], sc_ref[0, 2:3], sc_ref[0, 3:4]

    def body(n, carry):
        xf, xfs, xb, xbs = carry
        rowf = pl.multiple_of(n * bsz, bsz)
        nb = jnp.where(n < ncc, ncc - 1 - n, nch - 1 - (n - ncc))
        rowb = pl.multiple_of(nb * bsz, bsz)
        xs[pl.ds(rowf, bsz), 0:p2] = xf
        xs[pl.ds(rowb, bsz), p2:] = xb
        nxf = a_f * xf + s_f * xfs + inj[pl.ds(rowf, bsz), 0:p2]
        nxfs = a_f * xfs - s_f * xf + injs[pl.ds(rowf, bsz), 0:p2]
        nxb = a_b * xb + s_b * xbs + inj[pl.ds(rowb, bsz), p2:]
        nxbs = a_b * xbs - s_b * xb + injs[pl.ds(rowb, bsz), p2:]
        return nxf, nxfs, nxb, nxbs

    z = jnp.zeros((bsz, p2), F32)
    lax.fori_loop(0, nch, body, (z, z, z, z))
    y_ref[0] += _dot(xs[...].astype(BF16), cp_ref[0])


def _s5_core(ug, bsz, w, cp, scan_c, ctx_len):
    g, nch, _ = ug.shape
    b, bw, tc = bsz, BRANCH_W, S5_CHUNK
    rows = nch * b
    ug = ug.reshape(g, rows, bw)
    y = pl.pallas_call(
        functools.partial(_s5_kernel, bsz=b, nch=nch, ncc=ctx_len // tc),
        out_shape=jax.ShapeDtypeStruct((g, rows, bw), F32),
        grid=(g,),
        in_specs=[
            pl.BlockSpec((1, rows, bw), lambda i: (i, 0, 0)),
            pl.BlockSpec((1, bw, bw + 4 * S5_STATE), lambda i: (i, 0, 0)),
            pl.BlockSpec((1, 4 * S5_STATE, bw), lambda i: (i, 0, 0)),
            pl.BlockSpec((1, 8, 2 * S5_STATE), lambda i: (i, 0, 0)),
        ],
        out_specs=pl.BlockSpec((1, rows, bw), lambda i: (i, 0, 0)),
        scratch_shapes=[pltpu.VMEM((rows, 4 * S5_STATE), F32)] * 3,
        compiler_params=pltpu.CompilerParams(dimension_semantics=("parallel",), vmem_limit_bytes=VMEM_LIMIT),
        name="s5_chunked",
    )(ug, w, cp, scan_c)
    return y.reshape(g, nch, b * bw)


def _merge_kernel(h_ref, g1_ref, mod_ref, ya_ref, yb_ref, ys_ref, u_ref, yd_ref, s5p_ref, gluw_ref,
                  wg_ref, wb_ref, wo_ref, g2_ref, rwh_ref, rwl_ref, rb_ref,
                  hout_ref, f_ref, lg_ref):
    x = h_ref[0]
    mod = mod_ref[0, 0]
    a = _norm_modulate(x, g1_ref[...], mod, 0, 1)
    gates = _dot(a.astype(BF16), wg_ref[...])
    ys_tok = pltpu.einshape("gn(tc)->(nt)(gc)", ys_ref[...], t=S5_CHUNK, c=S5_GROUP_CH)
    z = ys_tok + s5p_ref[0:1] * u_ref[0]
    z = jax.nn.gelu(z)
    yc = z * jax.nn.sigmoid(_dot(z.astype(BF16), gluw_ref[...]) + s5p_ref[1:2])
    ys = (ya_ref[0], yb_ref[0], yc.astype(BF16), yd_ref[0])
    merged = None
    for i in range(N_BRANCH):
        term = jax.nn.sigmoid(gates[:, i * D_MODEL:(i + 1) * D_MODEL]) * _dot(ys[i], wb_ref[i])
        merged = term if merged is None else merged + term
    hn = x + mod[2:3] * _dot(merged.astype(BF16), wo_ref[...])
    hout_ref[0] = hn
    f = _norm_modulate(hn, g2_ref[...], mod, 3, 4)
    f_ref[0] = f
    fh, fl = _split_bf16(f)
    lg_ref[0] = _dot(fh, rwh_ref[...]) + _dot(fl, rwh_ref[...]) + _dot(fh, rwl_ref[...]) + rb_ref[...]


def _merge(hall, g1, modall, ya, yb, ys, u, yd, s5p, gluw, wg, wb, wo, g2, rwh, rwl, rbias, nct):
    b, t, _ = hall.shape
    tl = ROW_TILE
    row = lambda w: pl.BlockSpec((1, tl, w), lambda b, t: (b, t, 0))
    return pl.pallas_call(
        _merge_kernel,
        out_shape=(
            jax.ShapeDtypeStruct((b, t, D_MODEL), F32),
            jax.ShapeDtypeStruct((b, t, D_MODEL), F32),
            jax.ShapeDtypeStruct((b, t, LOGIT_W), F32),
        ),
        grid=(b, t // tl),
        in_specs=[
            row(D_MODEL), _const_spec((1, D_MODEL)), _mod_spec(nct),
            row(BRANCH_W), row(BRANCH_W),
            pl.BlockSpec((S5_GROUPS, tl // S5_CHUNK, BRANCH_W), lambda b, t: (0, t, b)),
            row(BRANCH_W), row(BRANCH_W),
            _const_spec((8, BRANCH_W)), _const_spec((BRANCH_W, BRANCH_W)),
            _const_spec((D_MODEL, N_BRANCH * D_MODEL)), _const_spec((N_BRANCH, BRANCH_W, D_MODEL)),
            _const_spec((D_MODEL, D_MODEL)), _const_spec((1, D_MODEL)),
            _const_spec((D_MODEL, LOGIT_W)), _const_spec((D_MODEL, LOGIT_W)), _const_spec((1, LOGIT_W)),
        ],
        out_specs=(row(D_MODEL), row(D_MODEL), row(LOGIT_W)),
        compiler_params=pltpu.CompilerParams(dimension_semantics=("parallel", "parallel"),
                                             vmem_limit_bytes=VMEM_LIMIT),
        name="merge_router",
    )(hall, g1, modall, ya, yb, ys, u, yd, s5p, gluw, wg, wb, wo, g2, rwh, rwl, rbias)


def _route_kernel(lg_ref, idx_ref, gate_ref):
    x = lg_ref[...]
    lane = lax.broadcasted_iota(jnp.int32, x.shape, 1).astype(F32)
    big = 1e9
    ninf = -jnp.inf
    in_g = lane < MOE_GROUPS
    m1 = jnp.max(jnp.where(in_g, x, ninf), axis=-1, keepdims=True)
    grp = jnp.min(jnp.where(in_g & (x == m1), lane, big), axis=-1, keepdims=True)
    p_grp = 1.0 / jnp.sum(jnp.where(in_g, jnp.exp(x - m1), 0.0), axis=-1, keepdims=True)
    lo = MOE_GROUPS + grp * EXPERTS_PER_GROUP
    in_e = (lane >= lo) & (lane < lo + EXPERTS_PER_GROUP)
    v0 = jnp.max(jnp.where(in_e, x, ninf), axis=-1, keepdims=True)
    i0 = jnp.min(jnp.where(in_e & (x == v0), lane, big), axis=-1, keepdims=True)
    rest = in_e & (lane != i0)
    v1 = jnp.max(jnp.where(rest, x, ninf), axis=-1, keepdims=True)
    i1 = jnp.min(jnp.where(rest & (x == v1), lane, big), axis=-1, keepdims=True)
    e1 = jnp.exp(v1 - v0)
    g0 = p_grp / (1.0 + e1)
    g1 = p_grp * e1 / (1.0 + e1)
    idx = jnp.where(lane == 0.0, i0 - MOE_GROUPS, jnp.where(lane == 1.0, i1 - MOE_GROUPS, 0.0))
    idx_ref[...] = idx.astype(jnp.int32)
    gate_ref[...] = jnp.where(lane == 0.0, g0, jnp.where(lane == 1.0, g1, 0.0))


def _route(logits):
    n = logits.shape[0]
    tl = ROW_TILE
    spec = pl.BlockSpec((tl, LOGIT_W), lambda i: (i, 0))
    return pl.pallas_call(
        _route_kernel,
        out_shape=(jax.ShapeDtypeStruct((n, LOGIT_W), jnp.int32), jax.ShapeDtypeStruct((n, LOGIT_W), F32)),
        grid=(n // tl,),
        in_specs=[spec],
        out_specs=(spec, spec),
        compiler_params=pltpu.CompilerParams(dimension_semantics=("parallel",)),
        name="moe_route",
    )(logits)


def _dispatch_plan(expert):
    n = expert.shape[0]
    mb = MOE_ROWS
    e_flat = expert.reshape(-1)
    onehot = (e_flat[:, None] == jnp.arange(N_EXPERTS, dtype=jnp.int32)[None, :]).astype(jnp.int32)
    seg = 512
    oh3 = onehot.reshape(-1, seg, N_EXPERTS).astype(BF16)
    tril = jnp.asarray(np.tril(np.ones((seg, seg), np.float32)), BF16)
    within = jnp.einsum('st,ntc->nsc', tril, oh3, preferred_element_type=F32)
    seg_tot = within[:, -1, :]
    seg_base = jnp.cumsum(seg_tot, axis=0) - seg_tot
    csum = (within + seg_base[:, None, :]).astype(jnp.int32).reshape(-1, N_EXPERTS)
    counts = csum[-1]
    padded = (counts + mb - 1) // mb * mb
    pad_ends = jnp.cumsum(padded)
    pad_starts = pad_ends - padded
    dest = jnp.sum(onehot * (csum - 1 + pad_starts[None, :]), axis=1)
    n_blocks = -(-(2 * n + N_EXPERTS * (mb - 1)) // mb)
    tok = jnp.repeat(jnp.arange(n, dtype=jnp.int32), 2)
    buf_tok = jnp.zeros((n_blocks * mb,), jnp.int32).at[dest].set(tok)
    blk_start = jnp.arange(n_blocks, dtype=jnp.int32) * mb
    blk_expert = jnp.minimum(jnp.sum((pad_ends[None, :] <= blk_start[:, None]).astype(jnp.int32), axis=1),
                             N_EXPERTS - 1).astype(jnp.int32)
    return dest.astype(jnp.int32), buf_tok.reshape(n_blocks, 1, mb), blk_expert


def _row_gather_start(idx_ref, src_hbm, dst, sem, nrows):
    for r in range(nrows):
        pltpu.make_async_copy(src_hbm.at[pl.ds(idx_ref[0, 0, r], 1)], dst.at[pl.ds(r, 1)], sem).start(priority=r % 2)


def _row_gather_wait(src_hbm, dst, sem, nrows):
    pltpu.make_async_copy(src_hbm.at[pl.ds(0, nrows)], dst, sem).wait()


def _ffn_kernel(be_ref, tok_ref, tokn_ref, f_hbm, w1_ref, w3_ref, w2_ref, y_ref, xbuf, sem):
    i = pl.program_id(0)
    nblk = pl.num_programs(0)
    slot = lax.rem(i, 2)

    @pl.when(i == 0)
    def _():
        _row_gather_start(tok_ref, f_hbm, xbuf.at[0], sem.at[0], MOE_ROWS)

    _row_gather_start(tokn_ref, f_hbm, xbuf.at[1 - slot], sem.at[1 - slot], MOE_ROWS)
    _row_gather_wait(f_hbm, xbuf.at[slot], sem.at[slot], MOE_ROWS)
    x = xbuf[slot].astype(BF16)
    a = _dot(x, w1_ref[0])
    hid = a * jax.nn.sigmoid(a) * _dot(x, w3_ref[0])
    y_ref[...] = _dot(hid.astype(BF16), w2_ref[0])

    @pl.when(i == nblk - 1)
    def _():
        _row_gather_wait(f_hbm, xbuf.at[1 - slot], sem.at[1 - slot], MOE_ROWS)


def _expert_ffn(f, buf_tok, blk_expert, w1, w3, w2):
    n_blocks = buf_tok.shape[0]
    mb = MOE_ROWS
    idx_spec = lambda off: pl.BlockSpec((1, 1, mb), lambda i, be: (jnp.minimum(i + off, n_blocks - 1), 0, 0),
                                        memory_space=pltpu.SMEM)
    return pl.pallas_call(
        _ffn_kernel,
        out_shape=jax.ShapeDtypeStruct((n_blocks * mb, D_MODEL), F32),
        grid_spec=pltpu.PrefetchScalarGridSpec(
            num_scalar_prefetch=1,
            grid=(n_blocks,),
            in_specs=[
                idx_spec(0), idx_spec(1),
                pl.BlockSpec(memory_space=pl.ANY),
                pl.BlockSpec((1, D_MODEL, D_FF_EXPERT), lambda i, be: (be[i], 0, 0)),
                pl.BlockSpec((1, D_MODEL, D_FF_EXPERT), lambda i, be: (be[i], 0, 0)),
                pl.BlockSpec((1, D_FF_EXPERT, D_MODEL), lambda i, be: (be[i], 0, 0)),
            ],
            out_specs=pl.BlockSpec((mb, D_MODEL), lambda i, be: (i, 0)),
            scratch_shapes=[pltpu.VMEM((2, mb, D_MODEL), F32), pltpu.SemaphoreType.DMA((2,))],
        ),
        compiler_params=pltpu.CompilerParams(dimension_semantics=("arbitrary",), vmem_limit_bytes=VMEM_LIMIT),
        name="moe_expert_ffn",
    )(blk_expert, buf_tok, buf_tok, f, w1, w3, w2)


def _combine_kernel(d_ref, dn_ref, h_ref, mod_ref, gate_ref, y_hbm, o_ref, rbuf, sem):
    i = pl.program_id(0)
    nsteps = pl.num_programs(0)
    slot = lax.rem(i, 2)
    tl = ROW_TILE

    @pl.when(i == 0)
    def _():
        _row_gather_start(d_ref, y_hbm, rbuf.at[0], sem.at[0], 2 * tl)

    _row_gather_start(dn_ref, y_hbm, rbuf.at[1 - slot], sem.at[1 - slot], 2 * tl)
    _row_gather_wait(y_hbm, rbuf.at[slot], sem.at[slot], 2 * tl)
    g = gate_ref[...]
    out = g[:, 0:1] * rbuf[slot, 0:tl, :] + g[:, 1:2] * rbuf[slot, tl:2 * tl, :]
    o_ref[...] = h_ref[...] + mod_ref[0, 0][5:6] * out

    @pl.when(i == nsteps - 1)
    def _():
        _row_gather_wait(y_hbm, rbuf.at[1 - slot], sem.at[1 - slot], 2 * tl)


def _combine(hflat, modall, gate, dest, yb, tiles_per_sample, nct):
    n = hflat.shape[0]
    tl = ROW_TILE
    nsteps = n // tl
    d2 = dest.reshape(nsteps, tl, 2).transpose(0, 2, 1).reshape(nsteps, 1, 2 * tl)
    idx_spec = lambda off: pl.BlockSpec((1, 1, 2 * tl), lambda i: (jnp.minimum(i + off, nsteps - 1), 0, 0),
                                        memory_space=pltpu.SMEM)
    row = lambda w: pl.BlockSpec((tl, w), lambda i: (i, 0))

    def mod_map(i):
        return (i // tiles_per_sample, jnp.where(i % tiles_per_sample >= nct, 1, 0), 0, 0)

    return pl.pallas_call(
        _combine_kernel,
        out_shape=jax.ShapeDtypeStruct((n, D_MODEL), F32),
        grid=(nsteps,),
        in_specs=[
            idx_spec(0), idx_spec(1), row(D_MODEL),
            pl.BlockSpec((1, 1, 8, D_MODEL), mod_map),
            row(LOGIT_W), pl.BlockSpec(memory_space=pl.ANY),
        ],
        out_specs=row(D_MODEL),
        scratch_shapes=[pltpu.VMEM((2, 2 * tl, D_MODEL), F32), pltpu.SemaphoreType.DMA((2,))],
        compiler_params=pltpu.CompilerParams(dimension_semantics=("arbitrary",), vmem_limit_bytes=VMEM_LIMIT),
        name="moe_combine",
    )(d2, d2, hflat, modall, gate, yb)


def _rope_tables(lat_len, ctx_len):
    pos = jnp.arange(lat_len, dtype=jnp.int32)
    n_freq = HEAD_DIM // 4
    inv = ROPE_BASE ** (-jnp.arange(n_freq, dtype=F32) / n_freq)
    ang_r = (pos // GRID_W).astype(F32)[:, None] * inv
    ang_c = (pos % GRID_W).astype(F32)[:, None] * inv
    zero = jnp.zeros_like(ang_r)
    cos = jnp.concatenate([jnp.cos(ang_r)] * 2 + [jnp.cos(ang_c)] * 2, axis=-1)
    sin_up = jnp.concatenate([-jnp.sin(ang_r), zero, -jnp.sin(ang_c), zero], axis=-1)
    sin_dn = jnp.concatenate([zero, jnp.sin(ang_r), zero, jnp.sin(ang_c)], axis=-1)

    def full(tab, ctx_val):
        tab = jnp.tile(tab, (1, N_HEADS))
        return jnp.concatenate([jnp.full((ctx_len, BRANCH_W), ctx_val, F32), tab], axis=0)

    return full(cos, 1.0), full(sin_up, 0.0), full(sin_dn, 0.0)


def kernel(x, c, ctx, c_ctx, mod_w, mod_b, norm1_g, norm2_g, w_in, w_branch, w_out, ret_log_decay,
           na_qk_gain, na_rpb, s5_lambda_re, s5_lambda_im, s5_log_step, s5_b_re, s5_b_im, s5_c_re,
           s5_c_im, s5_d, s5_glu_w, s5_glu_b, gqa_qk_gain, gqa_sink, router_w1, router_b1, router_w2,
           router_b2, exp_w1, exp_w3, exp_w2):
    depth = mod_w.shape[0]
    bsz, lat_len, _ = x.shape
    ctx_len = ctx.shape[1]
    t = ctx_len + lat_len
    nct = ctx_len // ROW_TILE
    assert ctx_len % ROW_TILE == 0 and lat_len % (NA_QROWS * GRID_W) == 0 and ctx_len % RET_CHUNK == 0
    assert lat_len // GRID_W >= NA_KROWS and lat_len >= GQA_KT

    hall = jnp.concatenate([ctx, x], axis=1)
    modall = _modulation(c, c_ctx, mod_w, mod_b)
    rope_tabs = _rope_tables(lat_len, ctx_len)
    ones_bd = jnp.asarray(np.kron(np.eye(N_HEADS), np.full((HEAD_DIM, HEAD_DIM), 1.0 / HEAD_DIM)), BF16)

    for i in range(depth):
        g1 = norm1_g[i].reshape(1, D_MODEL)
        g2 = norm2_g[i].reshape(1, D_MODEL)
        w_mix = w_in[i, :, :MIX_COLS].astype(BF16)
        w_gate = w_in[i, :, MIX_COLS:].astype(BF16)
        gains = jnp.concatenate([jnp.tile(na_qk_gain[i], (1, N_HEADS)), jnp.tile(gqa_qk_gain[i], (1, N_HEADS))], 0)
        ret, na, u, gqa, ug = _inproj(hall, g1, modall[i], w_mix, rope_tabs, gains, ones_bd, nct)

        dmask, rtabs, cdec = _ret_consts(ret_log_decay[i])
        ya = _retention(ret, dmask, rtabs, cdec, ctx_len // RET_CHUNK)
        yb = _neighborhood(na, _na_bias_tables(na_rpb[i], lat_len // GRID_W), ctx_len)
        s5w, s5cp, s5scan = _s5_consts(s5_lambda_re[i], s5_lambda_im[i], s5_log_step[i], s5_b_re[i], s5_b_im[i],
                                       s5_c_re[i], s5_c_im[i])
        ys = _s5_core(ug, bsz, s5w, s5cp, s5scan, ctx_len)
        yd = _window_gqa(gqa, gqa_sink[i], ctx_len)

        s5p = jnp.zeros((8, BRANCH_W), F32).at[0].set(s5_d[i]).at[1].set(s5_glu_b[i])
        rw = jnp.zeros((D_MODEL, LOGIT_W), F32).at[:, :MOE_GROUPS].set(router_w1[i])
        rw = rw.at[:, MOE_GROUPS:MOE_GROUPS + N_EXPERTS].set(router_w2[i])
        rwh, rwl = _split_bf16(rw)
        rbias = jnp.zeros((1, LOGIT_W), F32).at[0, :MOE_GROUPS].set(router_b1[i])
        rbias = rbias.at[0, MOE_GROUPS:MOE_GROUPS + N_EXPERTS].set(router_b2[i])
        hall, f, logits = _merge(hall, g1, modall[i], ya, yb, ys, u, yd, s5p, s5_glu_w[i].astype(BF16), w_gate,
                                 w_branch[i].astype(BF16), w_out[i].astype(BF16), g2, rwh, rwl, rbias, nct)

        n = bsz * t
        idx, gate = _route(logits.reshape(n, LOGIT_W))
        dest, buf_tok, blk_expert = _dispatch_plan(idx[:, :2])
        yexp = _expert_ffn(f.reshape(n, D_MODEL), buf_tok, blk_expert, exp_w1[i].astype(BF16),
                           exp_w3[i].astype(BF16), exp_w2[i].astype(BF16))
        hall = _combine(hall.reshape(n, D_MODEL), modall[i], gate, dest, yexp, t // ROW_TILE, nct)
        hall = hall.reshape(bsz, t, D_MODEL)

    return hall[:, ctx_len:]
```

```python
import functools
import math

import numpy as np
import jax
import jax.numpy as jnp
from jax import lax
from jax.experimental import pallas as pl
from jax.experimental.pallas import tpu as pltpu

F32 = jnp.float32
BF16 = jnp.bfloat16

D_MODEL = 1024
GRID_W = 64
HEAD_DIM = 64
N_BRANCH = 4
BRANCH_W = D_MODEL // 4
N_HEADS = BRANCH_W // HEAD_DIM
RET_CHUNK = 128
NA_KH = 8
NA_KW = 16
NA_QROWS = 4
NA_KROWS = NA_QROWS + NA_KH
S5_GROUP_CH = 16
S5_GROUPS = BRANCH_W // S5_GROUP_CH
S5_STATE = 64
S5_CHUNK = 16
GQA_KV_HEADS = 2
GQA_KV_W = GQA_KV_HEADS * HEAD_DIM
WIN = 128
MOE_GROUPS = 4
EXPERTS_PER_GROUP = 8
N_EXPERTS = MOE_GROUPS * EXPERTS_PER_GROUP
D_FF_EXPERT = D_MODEL // 2
ROPE_BASE = 10000.0
EPS = 1e-6
NEG = -1e30
MIX_COLS = 9 * BRANCH_W + 2 * GQA_KV_W

ROW_TILE = 256
MOE_ROWS = 256
LOGIT_W = 128
VMEM_LIMIT = 56 * 1024 * 1024


def _const_spec(shape):
    nd = len(shape)
    return pl.BlockSpec(shape, lambda *_: (0,) * nd)


def _dot(a, b):
    return jnp.dot(a, b, preferred_element_type=F32)


def _dot_nt(a, b):
    return lax.dot_general(a, b, (((1,), (1,)), ((), ())), preferred_element_type=F32)


def _dot_tn(a, b):
    return lax.dot_general(a, b, (((0,), (0,)), ((), ())), preferred_element_type=F32)


def _split_bf16(x):
    hi = x.astype(BF16)
    lo = (x - hi.astype(F32)).astype(BF16)
    return hi, lo


def _norm_modulate(x, gain, mod, shift_row, scale_row):
    ms = jnp.mean(x * x, axis=-1, keepdims=True)
    a = x * lax.rsqrt(ms + EPS) * gain
    return a * (1.0 + mod[scale_row:scale_row + 1]) + mod[shift_row:shift_row + 1]


def _mod_kernel(c_ref, w_ref, b_ref, o_ref):
    c = c_ref[...]
    s = c * jax.nn.sigmoid(c)
    o_ref[0] = _dot(s.astype(BF16), w_ref[0].astype(BF16)) + b_ref[0]


def _modulation(c, c_ctx, mod_w, mod_b):
    depth = mod_w.shape[0]
    b = c.shape[0]
    rows = -(-(b + 1) // 8) * 8
    cs = jnp.zeros((rows, D_MODEL), F32).at[:b].set(c).at[b].set(c_ctx)
    out = pl.pallas_call(
        _mod_kernel,
        out_shape=jax.ShapeDtypeStruct((depth, rows, 6 * D_MODEL), F32),
        grid=(depth, 6),
        in_specs=[
            _const_spec((rows, D_MODEL)),
            pl.BlockSpec((1, D_MODEL, D_MODEL), lambda l, j: (l, 0, j)),
            pl.BlockSpec((1, 1, D_MODEL), lambda l, j: (l, 0, j)),
        ],
        out_specs=pl.BlockSpec((1, rows, D_MODEL), lambda l, j: (l, 0, j)),
        compiler_params=pltpu.CompilerParams(dimension_semantics=("parallel", "parallel")),
        name="adaln_mod",
    )(cs, mod_w, mod_b.reshape(depth, 1, 6 * D_MODEL))
    lat = out[:, :b].reshape(depth, b, 6, D_MODEL)
    ctx = jnp.broadcast_to(out[:, b].reshape(depth, 1, 6, D_MODEL), (depth, b, 6, D_MODEL))
    modall = jnp.stack([ctx, lat], axis=2)
    return jnp.pad(modall, ((0, 0), (0, 0), (0, 0), (0, 2), (0, 0)))


def _head_mean_sq(x, ones_bd):
    hi, lo = _split_bf16(x * x)
    return _dot(hi, ones_bd) + _dot(lo, ones_bd)


def _rope(x, cos, sin_up, sin_dn):
    w = x.shape[-1]
    return x * cos + pltpu.roll(x, w - 16, 1) * sin_up + pltpu.roll(x, 16, 1) * sin_dn


def _inproj_kernel(h_ref, g1_ref, mod_ref, w_ref, cos_ref, sup_ref, sdn_ref, gains_ref, ones_ref,
                   ret_ref, na_ref, u_ref, gqa_ref, ug_ref):
    a = _norm_modulate(h_ref[0], g1_ref[...], mod_ref[0, 0], 0, 1)
    p = _dot(a.astype(BF16), w_ref[...])
    cos, sup, sdn = cos_ref[...], sup_ref[...], sdn_ref[...]
    ones_bd = ones_ref[...]
    scale = HEAD_DIM ** -0.5
    bw = BRANCH_W

    def col(i):
        return p[:, i * bw:(i + 1) * bw]

    def head_rms(x, gain):
        w = x.shape[-1]
        ms = _head_mean_sq(x, ones_bd[:w, :w])
        return x * lax.rsqrt(ms + EPS) * gain

    ret_ref[0, :, 0 * bw:1 * bw] = (_rope(col(0), cos, sup, sdn) * scale).astype(BF16)
    ret_ref[0, :, 1 * bw:2 * bw] = _rope(col(1), cos, sup, sdn).astype(BF16)
    ret_ref[0, :, 2 * bw:3 * bw] = col(2).astype(BF16)
    ret_ref[0, :, 3 * bw:4 * bw] = col(3).astype(BF16)
    na_ref[0, :, 0 * bw:1 * bw] = (head_rms(col(4), gains_ref[0:1]) * scale).astype(BF16)
    na_ref[0, :, 1 * bw:2 * bw] = head_rms(col(5), gains_ref[1:2]).astype(BF16)
    na_ref[0, :, 2 * bw:3 * bw] = col(6).astype(BF16)
    u_ref[0] = col(7)
    ug_ref[...] = pltpu.einshape("(nt)(gc)->gn(tc)", col(7), t=S5_CHUNK, g=S5_GROUPS).astype(BF16)
    gq = _rope(head_rms(col(8), gains_ref[2:3]), cos, sup, sdn) * scale
    gqa_ref[0, :, 0:bw] = gq.astype(BF16)
    kw = GQA_KV_W
    gk = head_rms(p[:, 9 * bw:9 * bw + kw], gains_ref[3:4, :kw])
    gk = _rope(gk, cos[:, :kw], sup[:, :kw], sdn[:, :kw])
    gqa_ref[0, :, bw:bw + kw] = gk.astype(BF16)
    gqa_ref[0, :, bw + kw:bw + 2 * kw] = p[:, 9 * bw + kw:9 * bw + 2 * kw].astype(BF16)


def _mod_spec(nct):
    return pl.BlockSpec((1, 1, 8, D_MODEL), lambda b, t: (b, jnp.where(t >= nct, 1, 0), 0, 0))


def _inproj(hall, g1, modall, w_mix, rope_tabs, gains, ones_bd, nct):
    b, t, _ = hall.shape
    tl = ROW_TILE
    row = lambda w: pl.BlockSpec((1, tl, w), lambda b, t: (b, t, 0))
    tab = pl.BlockSpec((tl, BRANCH_W), lambda b, t: (t, 0))
    return pl.pallas_call(
        _inproj_kernel,
        out_shape=(
            jax.ShapeDtypeStruct((b, t, 4 * BRANCH_W), BF16),
            jax.ShapeDtypeStruct((b, t, 3 * BRANCH_W), BF16),
            jax.ShapeDtypeStruct((b, t, BRANCH_W), F32),
            jax.ShapeDtypeStruct((b, t, BRANCH_W + 2 * GQA_KV_W), BF16),
            jax.ShapeDtypeStruct((S5_GROUPS, t // S5_CHUNK, b * BRANCH_W), BF16),
        ),
        grid=(b, t // tl),
        in_specs=[
            row(D_MODEL), _const_spec((1, D_MODEL)), _mod_spec(nct), _const_spec((D_MODEL, MIX_COLS)),
            tab, tab, tab, _const_spec((4, BRANCH_W)), _const_spec((BRANCH_W, BRANCH_W)),
        ],
        out_specs=(row(4 * BRANCH_W), row(3 * BRANCH_W), row(BRANCH_W), row(BRANCH_W + 2 * GQA_KV_W),
                   pl.BlockSpec((S5_GROUPS, tl // S5_CHUNK, BRANCH_W), lambda b, t: (0, t, b))),
        compiler_params=pltpu.CompilerParams(dimension_semantics=("parallel", "parallel"),
                                             vmem_limit_bytes=VMEM_LIMIT),
        name="inproj",
    )(hall, g1, modall, w_mix, *rope_tabs, gains, ones_bd)


def _ret_consts(log_decay):
    lg = -jnp.exp(log_decay.astype(F32))
    idx = jnp.arange(RET_CHUNK, dtype=F32)
    diff = idx[:, None] - idx[None, :]
    fwd = jnp.where(diff > 0, jnp.exp(jnp.maximum(diff, 0.0)[None] * lg[0][:, None, None]), 0.0)
    bwd = jnp.where(diff < 0, jnp.exp(jnp.maximum(-diff, 0.0)[None] * lg[1][:, None, None]), 0.0)
    dmask = fwd + bwd + 2.0 * jnp.eye(RET_CHUNK, dtype=F32)[None]
    tabs = jnp.stack([
        jnp.exp((RET_CHUNK - 1 - idx)[None, :] * lg[0][:, None]),
        jnp.exp(idx[None, :] * lg[1][:, None]),
        jnp.exp((idx + 1.0)[None, :] * lg[0][:, None]),
        jnp.exp((RET_CHUNK - idx)[None, :] * lg[1][:, None]),
    ])
    tabs = jnp.repeat(tabs.transpose(0, 2, 1), HEAD_DIM, axis=-1)
    cdec = jnp.exp(RET_CHUNK * lg).reshape(2 * N_HEADS)
    return dmask, tabs, cdec


def _ret_bwd_chunk(i, ncc, nch):
    return jnp.where(i < ncc, ncc - 1 - i, nch - 1 - (i - ncc))


def _ret_kernel(cdec_ref, q_ref, k_ref, v_ref, g_ref, dmask_ref, tab_ref, o_ref, sf, sb, sbprev, *, ncc):
    ph = pl.program_id(1)
    i = pl.program_id(2)
    nch = pl.num_programs(2)
    hd = HEAD_DIM

    @pl.when(jnp.logical_and(ph == 0, i == 0))
    def _():
        sb[...] = jnp.zeros_like(sb)

    @pl.when(jnp.logical_and(ph == 1, i == 0))
    def _():
        sf[...] = jnp.zeros_like(sf)

    k = k_ref[0]
    v = v_ref[0]

    @pl.when(ph == 0)
    def _():
        n = _ret_bwd_chunk(i, ncc, nch)
        kd = (k.astype(F32) * tab_ref[1]).astype(BF16)
        kv = _dot_tn(kd, v)
        for h in range(N_HEADS):
            sbprev[n, h] = sb[h]
            sb[h] = cdec_ref[N_HEADS + h] * sb[h] + kv[h * hd:(h + 1) * hd, h * hd:(h + 1) * hd]

    @pl.when(ph == 1)
    def _():
        q = q_ref[0]
        qf32 = q.astype(F32)
        qf = (qf32 * tab_ref[2]).astype(BF16)
        qb = (qf32 * tab_ref[3]).astype(BF16)
        kd = (k.astype(F32) * tab_ref[0]).astype(BF16)
        kv = _dot_tn(kd, v)
        outs = []
        for h in range(N_HEADS):
            hs = slice(h * hd, (h + 1) * hd)
            s = _dot_nt(q[:, hs], k[:, hs]) * dmask_ref[h]
            o = _dot(s.astype(BF16), v[:, hs])
            o = o + _dot(qf[:, hs], sf[h].astype(BF16)) + _dot(qb[:, hs], sbprev[i, h].astype(BF16))
            sf[h] = cdec_ref[h] * sf[h] + kv[hs, hs]
            outs.append(o * lax.rsqrt(jnp.mean(o * o, axis=-1, keepdims=True) + EPS))
        g = g_ref[0].astype(F32)
        o_ref[0] = (jnp.concatenate(outs, axis=-1) * (g * jax.nn.sigmoid(g))).astype(BF16)


def _retention(ret, dmask, tabs, cdec, ncc):
    b, t, _ = ret.shape
    nch = t // RET_CHUNK
    bw = BRANCH_W

    def kv_map(col):
        return lambda b, ph, i, *_: (b, jnp.where(ph == 0, _ret_bwd_chunk(i, ncc, nch), i), col)

    def q_map(col):
        return lambda b, ph, i, *_: (b, jnp.where(ph == 0, 0, i), col)

    blk = (1, RET_CHUNK, bw)
    return pl.pallas_call(
        functools.partial(_ret_kernel, ncc=ncc),
        out_shape=jax.ShapeDtypeStruct((b, t, bw), BF16),
        grid_spec=pltpu.PrefetchScalarGridSpec(
            num_scalar_prefetch=1,
            grid=(b, 2, nch),
            in_specs=[
                pl.BlockSpec(blk, q_map(0)), pl.BlockSpec(blk, kv_map(1)), pl.BlockSpec(blk, kv_map(2)),
                pl.BlockSpec(blk, q_map(3)),
                _const_spec((N_HEADS, RET_CHUNK, RET_CHUNK)), _const_spec((4, RET_CHUNK, bw)),
            ],
            out_specs=pl.BlockSpec(blk, q_map(0)),
            scratch_shapes=[
                pltpu.VMEM((N_HEADS, HEAD_DIM, HEAD_DIM), F32),
                pltpu.VMEM((N_HEADS, HEAD_DIM, HEAD_DIM), F32),
                pltpu.VMEM((nch, N_HEADS, HEAD_DIM, HEAD_DIM), F32),
            ],
        ),
        compiler_params=pltpu.CompilerParams(dimension_semantics=("parallel", "arbitrary", "arbitrary"),
                                             vmem_limit_bytes=VMEM_LIMIT),
        name="retention",
    )(cdec, ret, ret, ret, ret, dmask, tabs)


def _na_bias_tables(rpb, rows):
    nrb = rows // NA_QROWS
    qi = np.arange(NA_QROWS)
    kl = np.arange(NA_KROWS)
    ridx, rvalid = [], []
    for rb in (0, 1, nrb - 1):
        u0 = int(np.clip(NA_QROWS * rb - NA_KH // 2, 0, rows - NA_KROWS))
        r = NA_QROWS * rb + qi[:, None]
        kr = u0 + kl[None, :]
        r0 = np.clip(r - NA_KH // 2, 0, rows - NA_KH)
        rvalid.append((kr >= r0) & (kr < r0 + NA_KH))
        ridx.append(np.clip(kr - r + NA_KH - 1, 0, 2 * NA_KH - 2))
    ridx, rvalid = np.stack(ridx), np.stack(rvalid)
    qc = np.arange(GRID_W)[:, None]
    kc = np.arange(GRID_W)[None, :]
    ws = np.clip(qc - NA_KW // 2, 0, GRID_W - NA_KW)
    cvalid = (kc >= ws) & (kc < ws + NA_KW)
    cidx = np.clip(kc - qc + NA_KW - 1, 0, 2 * NA_KW - 2)
    row_sel = jnp.asarray(ridx[..., None] == np.arange(2 * NA_KH - 1), F32)
    col_sel = jnp.asarray(cidx[..., None] == np.arange(2 * NA_KW - 1), F32)
    hp = lax.Precision.HIGHEST
    rows_b = jnp.einsum('tikr,hrc->thikc', row_sel, rpb.astype(F32), precision=hp)
    bias = jnp.einsum('thikc,qzc->thiqkz', rows_b, col_sel, precision=hp)
    valid = rvalid[:, None, :, None, :, None] & cvalid[None, None, None, :, None, :]
    bias = jnp.where(jnp.asarray(valid), bias, NEG)
    return bias.reshape(3, N_HEADS, NA_QROWS * GRID_W, NA_KROWS * GRID_W)


def _softmax_pv(parts, sink=None):
    m = functools.reduce(jnp.maximum, [jnp.max(s, axis=-1, keepdims=True) for s, _ in parts])
    if sink is not None:
        m = jnp.maximum(m, sink)
    den = None
    acc = None
    for s, v in parts:
        p = jnp.exp(s - m)
        d = jnp.sum(p, axis=-1, keepdims=True)
        a = _dot(p.astype(BF16), v)
        den = d if den is None else den + d
        acc = a if acc is None else acc + a
    if sink is not None:
        den = den + jnp.exp(sink - m)
    return acc / den


def _na_kernel(q_ref, k_ref, v_ref, bias_ref, o_ref, *, nct, ctx_len, rows):
    j = pl.program_id(1)
    hd = HEAD_DIM
    q = q_ref[0]
    kc = k_ref[0, 0:ctx_len, :]
    vc = v_ref[0, 0:ctx_len, :]

    @pl.when(j < nct)
    def _():
        outs = []
        for h in range(N_HEADS):
            hs = slice(h * hd, (h + 1) * hd)
            outs.append(_softmax_pv([(_dot_nt(q[:, hs], kc[:, hs]), vc[:, hs])]))
        o_ref[0] = jnp.concatenate(outs, axis=-1).astype(BF16)

    @pl.when(j >= nct)
    def _():
        rb = j - nct
        u0 = jnp.clip(NA_QROWS * rb - NA_KH // 2, 0, rows - NA_KROWS)
        start = pl.multiple_of(ctx_len + u0 * GRID_W, GRID_W)
        kw = k_ref[0, pl.ds(start, NA_KROWS * GRID_W), :]
        vw = v_ref[0, pl.ds(start, NA_KROWS * GRID_W), :]
        outs = []
        for h in range(N_HEADS):
            hs = slice(h * hd, (h + 1) * hd)
            s_nb = _dot_nt(q[:, hs], kw[:, hs]) + bias_ref[0, h]
            s_cx = _dot_nt(q[:, hs], kc[:, hs])
            outs.append(_softmax_pv([(s_nb, vw[:, hs]), (s_cx, vc[:, hs])]))
        o_ref[0] = jnp.concatenate(outs, axis=-1).astype(BF16)


def _neighborhood(na, bias, ctx_len):
    b, t, _ = na.shape
    bw = BRANCH_W
    qt = NA_QROWS * GRID_W
    nct = ctx_len // qt
    nsteps = t // qt
    nrb = nsteps - nct
    rows = nrb * NA_QROWS

    def bias_map(b, j):
        rb = j - nct
        return (jnp.where(rb <= 0, 0, jnp.where(rb == nrb - 1, 2, 1)), 0, 0, 0)

    return pl.pallas_call(
        functools.partial(_na_kernel, nct=nct, ctx_len=ctx_len, rows=rows),
        out_shape=jax.ShapeDtypeStruct((b, t, bw), BF16),
        grid=(b, nsteps),
        in_specs=[
            pl.BlockSpec((1, qt, bw), lambda b, j: (b, j, 0)),
            pl.BlockSpec((1, t, bw), lambda b, j: (b, 0, 1)),
            pl.BlockSpec((1, t, bw), lambda b, j: (b, 0, 2)),
            pl.BlockSpec((1, N_HEADS, qt, NA_KROWS * GRID_W), bias_map),
        ],
        out_specs=pl.BlockSpec((1, qt, bw), lambda b, j: (b, j, 0)),
        compiler_params=pltpu.CompilerParams(dimension_semantics=("parallel", "arbitrary"),
                                             vmem_limit_bytes=VMEM_LIMIT),
        name="neighborhood_attn",
    )(na, na, na, bias)


GQA_QT = 256
GQA_KT = GQA_QT + 2 * WIN


def _gqa_kernel(sink_ref, q_ref, kv_ref, o_ref, *, nct, ctx_len, lat_len):
    j = pl.program_id(1)
    hd = HEAD_DIM
    grp = N_HEADS // GQA_KV_HEADS
    q = q_ref[0]
    kvc = kv_ref[0, 0:ctx_len, :]

    def stacked_q(kvh):
        return jnp.concatenate([q[:, (kvh * grp + g) * hd:(kvh * grp + g + 1) * hd] for g in range(grp)], axis=0)

    def sink_col(kvh):
        return jnp.concatenate([jnp.full((GQA_QT, 1), sink_ref[kvh * grp + g], F32) for g in range(grp)], axis=0)

    def emit(results):
        cols = [results[kvh][g * GQA_QT:(g + 1) * GQA_QT] for kvh in range(GQA_KV_HEADS) for g in range(grp)]
        o_ref[0] = jnp.concatenate(cols, axis=-1).astype(BF16)

    @pl.when(j < nct)
    def _():
        res = []
        for kvh in range(GQA_KV_HEADS):
            kc = kvc[:, kvh * hd:(kvh + 1) * hd]
            vc = kvc[:, GQA_KV_W + kvh * hd:GQA_KV_W + (kvh + 1) * hd]
            res.append(_softmax_pv([(_dot_nt(stacked_q(kvh), kc), vc)], sink=sink_col(kvh)))
        emit(res)

    @pl.when(j >= nct)
    def _():
        q0 = (j - nct) * GQA_QT
        k0 = jnp.clip(q0 - WIN, 0, lat_len - GQA_KT)
        kvw = kv_ref[0, pl.ds(pl.multiple_of(ctx_len + k0, WIN), GQA_KT), :]
        qpos = q0 + (lax.broadcasted_iota(jnp.int32, (grp * GQA_QT, GQA_KT), 0) & (GQA_QT - 1))
        kpos = k0 + lax.broadcasted_iota(jnp.int32, (grp * GQA_QT, GQA_KT), 1)
        band = jnp.abs(qpos - kpos) <= WIN
        res = []
        for kvh in range(GQA_KV_HEADS):
            ks = slice(kvh * hd, (kvh + 1) * hd)
            vs = slice(GQA_KV_W + kvh * hd, GQA_KV_W + (kvh + 1) * hd)
            q2 = stacked_q(kvh)
            s_w = jnp.where(band, _dot_nt(q2, kvw[:, ks]), NEG)
            s_c = _dot_nt(q2, kvc[:, ks])
            res.append(_softmax_pv([(s_w, kvw[:, vs]), (s_c, kvc[:, vs])], sink=sink_col(kvh)))
        emit(res)


def _window_gqa(gqa, sink, ctx_len):
    b, t, _ = gqa.shape
    bw = BRANCH_W
    nct = ctx_len // GQA_QT
    return pl.pallas_call(
        functools.partial(_gqa_kernel, nct=nct, ctx_len=ctx_len, lat_len=t - ctx_len),
        out_shape=jax.ShapeDtypeStruct((b, t, bw), BF16),
        grid_spec=pltpu.PrefetchScalarGridSpec(
            num_scalar_prefetch=1,
            grid=(b, t // GQA_QT),
            in_specs=[
                pl.BlockSpec((1, GQA_QT, bw), lambda b, j, *_: (b, j, 0)),
                pl.BlockSpec((1, t, bw), lambda b, j, *_: (b, 0, 1)),
            ],
            out_specs=pl.BlockSpec((1, GQA_QT, bw), lambda b, j, *_: (b, j, 0)),
        ),
        compiler_params=pltpu.CompilerParams(dimension_semantics=("parallel", "arbitrary"),
                                             vmem_limit_bytes=VMEM_LIMIT),
        name="window_gqa",
    )(sink.astype(F32), gqa, gqa)


def _cmul(ar, ai, br, bi):
    return ar * br - ai * bi, ar * bi + ai * br


def _s5_consts(lam_re, lam_im, log_step, b_re, b_im, c_re, c_im):
    f32 = F32
    tc = S5_CHUNK
    cre, cim = c_re.astype(f32), c_im.astype(f32)
    bre, bim = b_re.astype(f32), b_im.astype(f32)
    kern, inj, cpow, scan = [], [], [], []
    for d in range(2):
        lre, lim, dt = lam_re[d].astype(f32), lam_im[d].astype(f32), jnp.exp(log_step[d].astype(f32))[:, None]
        mag = jnp.exp(lre * dt)
        ar, ai = mag * jnp.cos(lim * dt), mag * jnp.sin(lim * dt)
        den = lre * lre + lim * lim
        nr = ar - 1.0
        coef_re = (nr * lre + ai * lim) / den
        coef_im = (ai * lre - nr * lim) / den
        bbr = coef_re[..., None] * bre - coef_im[..., None] * bim
        bbi = coef_re[..., None] * bim + coef_im[..., None] * bre
        pr, pi = [jnp.ones_like(ar)], [jnp.zeros_like(ar)]
        for _ in range(tc):
            nr_, ni_ = _cmul(pr[-1], pi[-1], ar, ai)
            pr.append(nr_)
            pi.append(ni_)
        pr, pi = jnp.stack(pr), jnp.stack(pi)
        lbr, lbi = _cmul(pr[:tc, :, :, None], pi[:tc, :, :, None], bbr[None], bbi[None])
        kern.append(jnp.einsum('gcp,tgpd->tgcd', cre, lbr, precision=lax.Precision.HIGHEST)
                    - jnp.einsum('gcp,tgpd->tgcd', cim, lbi, precision=lax.Precision.HIGHEST))
        inj.append((lbr[::-1], lbi[::-1]) if d == 0 else (lbr, lbi))
        pwr, pwi = (pr[1:], pi[1:]) if d == 0 else (pr[1:][::-1], pi[1:][::-1])
        dr, di = _cmul(cre[None], cim[None], pwr[:, :, None, :], pwi[:, :, None, :])
        cpow.append((dr, -di))
        scan.append((pr[tc], pi[tc]))
    g, gc, p = S5_GROUPS, S5_GROUP_CH, S5_STATE
    lag = np.arange(tc)[None, :] - np.arange(tc)[:, None]
    taus = np.arange(tc)[:, None, None]
    sel_f = jnp.asarray(lag[None] == taus, F32)
    sel_b = jnp.asarray(-lag[None] == taus, F32)
    hp = lax.Precision.HIGHEST
    toep = (jnp.einsum('xst,xgcd->gsdtc', sel_f, kern[0], precision=hp)
            + jnp.einsum('xst,xgcd->gsdtc', sel_b, kern[1], precision=hp))
    toep = toep.reshape(g, tc * gc, tc * gc)
    injm = jnp.concatenate([x.transpose(1, 0, 3, 2).reshape(g, tc * gc, p)
                            for x in (inj[0][0], inj[0][1], inj[1][0], inj[1][1])], axis=-1)
    w = jnp.concatenate([toep, injm], axis=-1)
    cp = jnp.concatenate([x.transpose(1, 3, 0, 2).reshape(g, p, tc * gc)
                          for x in (cpow[0][0], cpow[0][1], cpow[1][0], cpow[1][1])], axis=1)
    rows = []
    for d in range(2):
        sr, si = scan[d]
        rows += [jnp.concatenate([sr, sr], -1), jnp.concatenate([-si, si], -1)]
    scan_c = jnp.stack(rows, axis=1)
    scan_c = jnp.pad(scan_c, ((0, 0), (0, 4), (0, 0)))
    return w.astype(BF16), cp.astype(BF16), scan_c


def _s5_kernel(u_ref, w_ref, cp_ref, sc_ref, y_ref, inj, injs, xs, *, bsz, nch, ncc):
    p2 = 2 * S5_STATE
    r = _dot(u_ref[0], w_ref[0])
    y_ref[0] = r[:, :BRANCH_W]
    rf = r[:, BRANCH_W:BRANCH_W + p2]
    rb = r[:, BRANCH_W + p2:]
    inj[:, 0:p2] = rf
    inj[:, p2:] = rb
    injs[:, 0:p2] = pltpu.roll(rf, S5_STATE, 1)
    injs[:, p2:] = pltpu.roll(rb, S5_STATE, 1)
    a_f, s_f, a_b, s_b = sc_ref[0, 0:1], sc_ref[0, 1:2], sc_ref[0, 2:3], sc_ref[0, 3:4]

    def body(n, carry):
        xf, xfs, xb, xbs = carry
        rowf = pl.multiple_of(n * bsz, bsz)
        nb = jnp.where(n < ncc, ncc - 1 - n, nch - 1 - (n - ncc))
        rowb = pl.multiple_of(nb * bsz, bsz)
        xs[pl.ds(rowf, bsz), 0:p2] = xf
        xs[pl.ds(rowb, bsz), p2:] = xb
        nxf = a_f * xf + s_f * xfs + inj[pl.ds(rowf, bsz), 0:p2]
        nxfs = a_f * xfs - s_f * xf + injs[pl.ds(rowf, bsz), 0:p2]
        nxb = a_b * xb + s_b * xbs + inj[pl.ds(rowb, bsz), p2:]
        nxbs = a_b * xbs - s_b * xb + injs[pl.ds(rowb, bsz), p2:]
        return nxf, nxfs, nxb, nxbs

    z = jnp.zeros((bsz, p2), F32)
    lax.fori_loop(0, nch, body, (z, z, z, z))
    y_ref[0] += _dot(xs[...].astype(BF16), cp_ref[0])


def _s5_core(ug, bsz, w, cp, scan_c, ctx_len):
    g, nch, _ = ug.shape
    b, bw, tc = bsz, BRANCH_W, S5_CHUNK
    rows = nch * b
    ug = ug.reshape(g, rows, bw)
    y = pl.pallas_call(
        functools.partial(_s5_kernel, bsz=b, nch=nch, ncc=ctx_len // tc),
        out_shape=jax.ShapeDtypeStruct((g, rows, bw), F32),
        grid=(g,),
        in_specs=[
            pl.BlockSpec((1, rows, bw), lambda i: (i, 0, 0)),
            pl.BlockSpec((1, bw, bw + 4 * S5_STATE), lambda i: (i, 0, 0)),
            pl.BlockSpec((1, 4 * S5_STATE, bw), lambda i: (i, 0, 0)),
            pl.BlockSpec((1, 8, 2 * S5_STATE), lambda i: (i, 0, 0)),
        ],
        out_specs=pl.BlockSpec((1, rows, bw), lambda i: (i, 0, 0)),
        scratch_shapes=[pltpu.VMEM((rows, 4 * S5_STATE), F32)] * 3,
        compiler_params=pltpu.CompilerParams(dimension_semantics=("parallel",), vmem_limit_bytes=VMEM_LIMIT),
        name="s5_chunked",
    )(ug, w, cp, scan_c)
    return y.reshape(g, nch, b * bw)


def _merge_kernel(h_ref, g1_ref, mod_ref, ya_ref, yb_ref, ys_ref, u_ref, yd_ref, s5p_ref, gluw_ref,
                  wg_ref, wb_ref, wo_ref, g2_ref, rwh_ref, rwl_ref, rb_ref,
                  hout_ref, f_ref, lg_ref):
    x = h_ref[0]
    mod = mod_ref[0, 0]
    a = _norm_modulate(x, g1_ref[...], mod, 0, 1)
    gates = _dot(a.astype(BF16), wg_ref[...])
    ys_tok = pltpu.einshape("gn(tc)->(nt)(gc)", ys_ref[...], t=S5_CHUNK, c=S5_GROUP_CH)
    z = ys_tok + s5p_ref[0:1] * u_ref[0]
    z = jax.nn.gelu(z)
    yc = z * jax.nn.sigmoid(_dot(z.astype(BF16), gluw_ref[...]) + s5p_ref[1:2])
    ys = (ya_ref[0], yb_ref[0], yc.astype(BF16), yd_ref[0])
    merged = None
    for i in range(N_BRANCH):
        term = jax.nn.sigmoid(gates[:, i * D_MODEL:(i + 1) * D_MODEL]) * _dot(ys[i], wb_ref[i])
        merged = term if merged is None else merged + term
    hn = x + mod[2:3] * _dot(merged.astype(BF16), wo_ref[...])
    hout_ref[0] = hn
    f = _norm_modulate(hn, g2_ref[...], mod, 3, 4)
    f_ref[0] = f
    fh, fl = _split_bf16(f)
    lg_ref[0] = _dot(fh, rwh_ref[...]) + _dot(fl, rwh_ref[...]) + _dot(fh, rwl_ref[...]) + rb_ref[...]


def _merge(hall, g1, modall, ya, yb, ys, u, yd, s5p, gluw, wg, wb, wo, g2, rwh, rwl, rbias, nct):
    b, t, _ = hall.shape
    tl = ROW_TILE
    row = lambda w: pl.BlockSpec((1, tl, w), lambda b, t: (b, t, 0))
    return pl.pallas_call(
        _merge_kernel,
        out_shape=(
            jax.ShapeDtypeStruct((b, t, D_MODEL), F32),
            jax.ShapeDtypeStruct((b, t, D_MODEL), F32),
            jax.ShapeDtypeStruct((b, t, LOGIT_W), F32),
        ),
        grid=(b, t // tl),
        in_specs=[
            row(D_MODEL), _const_spec((1, D_MODEL)), _mod_spec(nct),
            row(BRANCH_W), row(BRANCH_W),
            pl.BlockSpec((S5_GROUPS, tl // S5_CHUNK, BRANCH_W), lambda b, t: (0, t, b)),
            row(BRANCH_W), row(BRANCH_W),
            _const_spec((8, BRANCH_W)), _const_spec((BRANCH_W, BRANCH_W)),
            _const_spec((D_MODEL, N_BRANCH * D_MODEL)), _const_spec((N_BRANCH, BRANCH_W, D_MODEL)),
            _const_spec((D_MODEL, D_MODEL)), _const_spec((1, D_MODEL)),
            _const_spec((D_MODEL, LOGIT_W)), _const_spec((D_MODEL, LOGIT_W)), _const_spec((1, LOGIT_W)),
        ],
        out_specs=(row(D_MODEL), row(D_MODEL), row(LOGIT_W)),
        compiler_params=pltpu.CompilerParams(dimension_semantics=("parallel", "parallel"),
                                             vmem_limit_bytes=VMEM_LIMIT),
        name="merge_router",
    )(hall, g1, modall, ya, yb, ys, u, yd, s5p, gluw, wg, wb, wo, g2, rwh, rwl, rbias)


def _route_kernel(lg_ref, idx_ref, gate_ref):
    x = lg_ref[...]
    lane = lax.broadcasted_iota(jnp.int32, x.shape, 1).astype(F32)
    big = 1e9
    ninf = -jnp.inf
    in_g = lane < MOE_GROUPS
    m1 = jnp.max(jnp.where(in_g, x, ninf), axis=-1, keepdims=True)
    grp = jnp.min(jnp.where(in_g & (x == m1), lane, big), axis=-1, keepdims=True)
    p_grp = 1.0 / jnp.sum(jnp.where(in_g, jnp.exp(x - m1), 0.0), axis=-1, keepdims=True)
    lo = MOE_GROUPS + grp * EXPERTS_PER_GROUP
    in_e = (lane >= lo) & (lane < lo + EXPERTS_PER_GROUP)
    v0 = jnp.max(jnp.where(in_e, x, ninf), axis=-1, keepdims=True)
    i0 = jnp.min(jnp.where(in_e & (x == v0), lane, big), axis=-1, keepdims=True)
    rest = in_e & (lane != i0)
    v1 = jnp.max(jnp.where(rest, x, ninf), axis=-1, keepdims=True)
    i1 = jnp.min(jnp.where(rest & (x == v1), lane, big), axis=-1, keepdims=True)
    e1 = jnp.exp(v1 - v0)
    g0 = p_grp / (1.0 + e1)
    g1 = p_grp * e1 / (1.0 + e1)
    idx = jnp.where(lane == 0.0, i0 - MOE_GROUPS, jnp.where(lane == 1.0, i1 - MOE_GROUPS, 0.0))
    idx_ref[...] = idx.astype(jnp.int32)
    gate_ref[...] = jnp.where(lane == 0.0, g0, jnp.where(lane == 1.0, g1, 0.0))


def _route(logits):
    n = logits.shape[0]
    tl = next(r for r in (4 * ROW_TILE, 2 * ROW_TILE, ROW_TILE) if n % r == 0)
    spec = pl.BlockSpec((tl, LOGIT_W), lambda i: (i, 0))
    return pl.pallas_call(
        _route_kernel,
        out_shape=(jax.ShapeDtypeStruct((n, LOGIT_W), jnp.int32), jax.ShapeDtypeStruct((n, LOGIT_W), F32)),
        grid=(n // tl,),
        in_specs=[spec],
        out_specs=(spec, spec),
        compiler_params=pltpu.CompilerParams(dimension_semantics=("parallel",)),
        name="moe_route",
    )(logits)


def _dispatch_plan(expert):
    n = expert.shape[0]
    mb = MOE_ROWS
    e_flat = expert.reshape(-1)
    onehot = (e_flat[:, None] == jnp.arange(N_EXPERTS, dtype=jnp.int32)[None, :]).astype(jnp.int32)
    seg = 512
    oh3 = onehot.reshape(-1, seg, N_EXPERTS).astype(BF16)
    tril = jnp.asarray(np.tril(np.ones((seg, seg), np.float32)), BF16)
    within = jnp.einsum('st,ntc->nsc', tril, oh3, preferred_element_type=F32)
    seg_tot = within[:, -1, :]
    seg_base = jnp.cumsum(seg_tot, axis=0) - seg_tot
    csum = (within + seg_base[:, None, :]).astype(jnp.int32).reshape(-1, N_EXPERTS)
    counts = csum[-1]
    padded = (counts + mb - 1) // mb * mb
    pad_ends = jnp.cumsum(padded)
    pad_starts = pad_ends - padded
    dest = jnp.sum(onehot * (csum - 1 + pad_starts[None, :]), axis=1)
    n_blocks = -(-(2 * n + N_EXPERTS * (mb - 1)) // mb)
    tok = jnp.repeat(jnp.arange(n, dtype=jnp.int32), 2)
    buf_tok = jnp.zeros((n_blocks * mb,), jnp.int32).at[dest].set(tok, unique_indices=True)
    blk_start = jnp.arange(n_blocks, dtype=jnp.int32) * mb
    blk_expert = jnp.minimum(jnp.sum((pad_ends[None, :] <= blk_start[:, None]).astype(jnp.int32), axis=1),
                             N_EXPERTS - 1).astype(jnp.int32)
    return dest.astype(jnp.int32), buf_tok.reshape(n_blocks, 1, mb), blk_expert


def _row_gather_start(idx_ref, src_hbm, dst, sem, nrows):
    for r in range(nrows):
        pltpu.make_async_copy(src_hbm.at[pl.ds(idx_ref[0, 0, r], 1)], dst.at[pl.ds(r, 1)], sem).start(priority=r % 2)


def _row_gather_wait(src_hbm, dst, sem, nrows):
    pltpu.make_async_copy(src_hbm.at[pl.ds(0, nrows)], dst, sem).wait()


def _ffn_kernel(be_ref, tok_ref, tokn_ref, f_hbm, w1_ref, w3_ref, w2_ref, y_ref, xbuf, sem):
    i = pl.program_id(0)
    nblk = pl.num_programs(0)
    slot = lax.rem(i, 2)

    @pl.when(i == 0)
    def _():
        _row_gather_start(tok_ref, f_hbm, xbuf.at[0], sem.at[0], MOE_ROWS)

    _row_gather_start(tokn_ref, f_hbm, xbuf.at[1 - slot], sem.at[1 - slot], MOE_ROWS)
    _row_gather_wait(f_hbm, xbuf.at[slot], sem.at[slot], MOE_ROWS)
    x = xbuf[slot].astype(BF16)
    a = _dot(x, w1_ref[0])
    hid = a * jax.nn.sigmoid(a) * _dot(x, w3_ref[0])
    y_ref[...] = _dot(hid.astype(BF16), w2_ref[0])

    @pl.when(i == nblk - 1)
    def _():
        _row_gather_wait(f_hbm, xbuf.at[1 - slot], sem.at[1 - slot], MOE_ROWS)


def _expert_ffn(f, buf_tok, blk_expert, w1, w3, w2):
    n_blocks = buf_tok.shape[0]
    mb = MOE_ROWS
    idx_spec = lambda off: pl.BlockSpec((1, 1, mb), lambda i, be: (jnp.minimum(i + off, n_blocks - 1), 0, 0),
                                        memory_space=pltpu.SMEM)
    return pl.pallas_call(
        _ffn_kernel,
        out_shape=jax.ShapeDtypeStruct((n_blocks * mb, D_MODEL), F32),
        grid_spec=pltpu.PrefetchScalarGridSpec(
            num_scalar_prefetch=1,
            grid=(n_blocks,),
            in_specs=[
                idx_spec(0), idx_spec(1),
                pl.BlockSpec(memory_space=pl.ANY),
                pl.BlockSpec((1, D_MODEL, D_FF_EXPERT), lambda i, be: (be[i], 0, 0)),
                pl.BlockSpec((1, D_MODEL, D_FF_EXPERT), lambda i, be: (be[i], 0, 0)),
                pl.BlockSpec((1, D_FF_EXPERT, D_MODEL), lambda i, be: (be[i], 0, 0)),
            ],
            out_specs=pl.BlockSpec((mb, D_MODEL), lambda i, be: (i, 0)),
            scratch_shapes=[pltpu.VMEM((2, mb, D_MODEL), F32), pltpu.SemaphoreType.DMA((2,))],
        ),
        compiler_params=pltpu.CompilerParams(dimension_semantics=("arbitrary",), vmem_limit_bytes=VMEM_LIMIT),
        name="moe_expert_ffn",
    )(blk_expert, buf_tok, buf_tok, f, w1, w3, w2)


def _combine_kernel(d_ref, dn_ref, h_ref, mod_ref, gate_ref, y_hbm, o_ref, rbuf, sem):
    i = pl.program_id(0)
    nsteps = pl.num_programs(0)
    slot = lax.rem(i, 2)
    tl = ROW_TILE

    @pl.when(i == 0)
    def _():
        _row_gather_start(d_ref, y_hbm, rbuf.at[0], sem.at[0], 2 * tl)

    _row_gather_start(dn_ref, y_hbm, rbuf.at[1 - slot], sem.at[1 - slot], 2 * tl)
    _row_gather_wait(y_hbm, rbuf.at[slot], sem.at[slot], 2 * tl)
    g = gate_ref[...]
    out = g[:, 0:1] * rbuf[slot, 0:tl, :] + g[:, 1:2] * rbuf[slot, tl:2 * tl, :]
    o_ref[...] = h_ref[...] + mod_ref[0, 0][5:6] * out

    @pl.when(i == nsteps - 1)
    def _():
        _row_gather_wait(y_hbm, rbuf.at[1 - slot], sem.at[1 - slot], 2 * tl)


def _combine(hflat, modall, gate, dest, yb, tiles_per_sample, nct):
    n = hflat.shape[0]
    tl = ROW_TILE
    nsteps = n // tl
    d2 = dest.reshape(nsteps, tl, 2).transpose(0, 2, 1).reshape(nsteps, 1, 2 * tl)
    idx_spec = lambda off: pl.BlockSpec((1, 1, 2 * tl), lambda i: (jnp.minimum(i + off, nsteps - 1), 0, 0),
                                        memory_space=pltpu.SMEM)
    row = lambda w: pl.BlockSpec((tl, w), lambda i: (i, 0))

    def mod_map(i):
        return (i // tiles_per_sample, jnp.where(i % tiles_per_sample >= nct, 1, 0), 0, 0)

    return pl.pallas_call(
        _combine_kernel,
        out_shape=jax.ShapeDtypeStruct((n, D_MODEL), F32),
        grid=(nsteps,),
        in_specs=[
            idx_spec(0), idx_spec(1), row(D_MODEL),
            pl.BlockSpec((1, 1, 8, D_MODEL), mod_map),
            row(LOGIT_W), pl.BlockSpec(memory_space=pl.ANY),
        ],
        out_specs=row(D_MODEL),
        scratch_shapes=[pltpu.VMEM((2, 2 * tl, D_MODEL), F32), pltpu.SemaphoreType.DMA((2,))],
        compiler_params=pltpu.CompilerParams(dimension_semantics=("arbitrary",), vmem_limit_bytes=VMEM_LIMIT),
        name="moe_combine",
    )(d2, d2, hflat, modall, gate, yb)


def _rope_tables(lat_len, ctx_len):
    pos = jnp.arange(lat_len, dtype=jnp.int32)
    n_freq = HEAD_DIM // 4
    inv = ROPE_BASE ** (-jnp.arange(n_freq, dtype=F32) / n_freq)
    ang_r = (pos // GRID_W).astype(F32)[:, None] * inv
    ang_c = (pos % GRID_W).astype(F32)[:, None] * inv
    zero = jnp.zeros_like(ang_r)
    cos = jnp.concatenate([jnp.cos(ang_r)] * 2 + [jnp.cos(ang_c)] * 2, axis=-1)
    sin_up = jnp.concatenate([-jnp.sin(ang_r), zero, -jnp.sin(ang_c), zero], axis=-1)
    sin_dn = jnp.concatenate([zero, jnp.sin(ang_r), zero, jnp.sin(ang_c)], axis=-1)

    def full(tab, ctx_val):
        tab = jnp.tile(tab, (1, N_HEADS))
        return jnp.concatenate([jnp.full((ctx_len, BRANCH_W), ctx_val, F32), tab], axis=0)

    return full(cos, 1.0), full(sin_up, 0.0), full(sin_dn, 0.0)


def kernel(x, c, ctx, c_ctx, mod_w, mod_b, norm1_g, norm2_g, w_in, w_branch, w_out, ret_log_decay,
           na_qk_gain, na_rpb, s5_lambda_re, s5_lambda_im, s5_log_step, s5_b_re, s5_b_im, s5_c_re,
           s5_c_im, s5_d, s5_glu_w, s5_glu_b, gqa_qk_gain, gqa_sink, router_w1, router_b1, router_w2,
           router_b2, exp_w1, exp_w3, exp_w2):
    depth = mod_w.shape[0]
    bsz, lat_len, _ = x.shape
    ctx_len = ctx.shape[1]
    t = ctx_len + lat_len
    nct = ctx_len // ROW_TILE
    assert ctx_len % ROW_TILE == 0 and lat_len % (NA_QROWS * GRID_W) == 0 and ctx_len % RET_CHUNK == 0
    assert lat_len // GRID_W >= NA_KROWS and lat_len >= GQA_KT

    hall = jnp.concatenate([ctx, x], axis=1)
    modall = _modulation(c, c_ctx, mod_w, mod_b)
    rope_tabs = _rope_tables(lat_len, ctx_len)
    ones_bd = jnp.asarray(np.kron(np.eye(N_HEADS), np.full((HEAD_DIM, HEAD_DIM), 1.0 / HEAD_DIM)), BF16)

    for i in range(depth):
        g1 = norm1_g[i].reshape(1, D_MODEL)
        g2 = norm2_g[i].reshape(1, D_MODEL)
        w_mix = w_in[i, :, :MIX_COLS].astype(BF16)
        w_gate = w_in[i, :, MIX_COLS:].astype(BF16)
        gains = jnp.concatenate([jnp.tile(na_qk_gain[i], (1, N_HEADS)), jnp.tile(gqa_qk_gain[i], (1, N_HEADS))], 0)
        ret, na, u, gqa, ug = _inproj(hall, g1, modall[i], w_mix, rope_tabs, gains, ones_bd, nct)

        dmask, rtabs, cdec = _ret_consts(ret_log_decay[i])
        ya = _retention(ret, dmask, rtabs, cdec, ctx_len // RET_CHUNK)
        yb = _neighborhood(na, _na_bias_tables(na_rpb[i], lat_len // GRID_W), ctx_len)
        s5w, s5cp, s5scan = _s5_consts(s5_lambda_re[i], s5_lambda_im[i], s5_log_step[i], s5_b_re[i], s5_b_im[i],
                                       s5_c_re[i], s5_c_im[i])
        ys = _s5_core(ug, bsz, s5w, s5cp, s5scan, ctx_len)
        yd = _window_gqa(gqa, gqa_sink[i], ctx_len)

        s5p = jnp.zeros((8, BRANCH_W), F32).at[0].set(s5_d[i]).at[1].set(s5_glu_b[i])
        rw = jnp.zeros((D_MODEL, LOGIT_W), F32).at[:, :MOE_GROUPS].set(router_w1[i])
        rw = rw.at[:, MOE_GROUPS:MOE_GROUPS + N_EXPERTS].set(router_w2[i])
        rwh, rwl = _split_bf16(rw)
        rbias = jnp.zeros((1, LOGIT_W), F32).at[0, :MOE_GROUPS].set(router_b1[i])
        rbias = rbias.at[0, MOE_GROUPS:MOE_GROUPS + N_EXPERTS].set(router_b2[i])
        hall, f, logits = _merge(hall, g1, modall[i], ya, yb, ys, u, yd, s5p, s5_glu_w[i].astype(BF16), w_gate,
                                 w_branch[i].astype(BF16), w_out[i].astype(BF16), g2, rwh, rwl, rbias, nct)

        n = bsz * t
        idx, gate = _route(logits.reshape(n, LOGIT_W))
        dest, buf_tok, blk_expert = _dispatch_plan(idx[:, :2])
        yexp = _expert_ffn(f.reshape(n, D_MODEL), buf_tok, blk_expert, exp_w1[i].astype(BF16),
                           exp_w3[i].astype(BF16), exp_w2[i].astype(BF16))
        hall = _combine(hall.reshape(n, D_MODEL), modall[i], gate, dest, yexp, t // ROW_TILE, nct)
        hall = hall.reshape(bsz, t, D_MODEL)

    return hall[:, ctx_len:]
```
